```python
import jax, jax.numpy as jnp
from jax import lax
import numpy as np

D_MODEL = 4096
BATCH = 8
SEQ = 2048
DEPTH = 2
DEC_BATCH = 8
DEC_SEQ = 64
PAST_LEN = 2048

CHUNK = 64
N_BRANCH = 4
D_BRANCH = D_MODEL // N_BRANCH
POOL_WINDOWS = (2, 4, 8, 16)
N_POOL_GROUPS = len(POOL_WINDOWS)
POOL_GROUP = D_BRANCH // N_POOL_GROUPS
POOL_STATE = max(POOL_WINDOWS) - 1
SCONV_W = 3
CCONV_W = 31
GMLP_CHUNK = 128
N_GMLP_HEADS = 8
GMLP_HEAD = D_BRANCH // N_GMLP_HEADS
D_FF = 4 * D_MODEL
EPS = 1e-6
OFF_POOL = 0
OFF_SC = OFF_POOL + D_BRANCH
OFF_CC = OFF_SC + 3 * D_BRANCH
OFF_GM = OFF_CC + 2 * D_BRANCH
OFF_GATE = OFF_GM + 2 * D_BRANCH
IN_COLS = OFF_GATE + N_BRANCH * D_MODEL

kernel_name = 'streaming_gated_hybrid_encoder_step'


def _rmsnorm(x, g):
    xf = x.astype(jnp.float32)
    y = xf * lax.rsqrt(jnp.mean(xf * xf, axis=-1, keepdims=True) + EPS)
    return (y * g.astype(jnp.float32)).astype(x.dtype)


def _layernorm(x, g, b):
    xf = x.astype(jnp.float32)
    mu = jnp.mean(xf, axis=-1, keepdims=True)
    xc = xf - mu
    y = xc * lax.rsqrt(jnp.mean(xc * xc, axis=-1, keepdims=True) + EPS)
    return (y * g.astype(jnp.float32) + b.astype(jnp.float32)).astype(x.dtype)


def _dwconv_causal(x_ext, w):
    return lax.conv_general_dilated(
        x_ext, w[:, None, :].astype(x_ext.dtype), window_strides=(1,), padding='VALID',
        dimension_numbers=('NWC', 'WIO', 'NWC'), feature_group_count=x_ext.shape[-1])


def _pool_mixer(p_ext, start_pos, w_pool, pool_scale):
    B, Text, C = p_ext.shape
    T = Text - POOL_STATE
    cs = jnp.cumsum(p_ext.astype(jnp.float32), axis=1)
    cs = jnp.pad(cs, ((0, 0), (1, 0), (0, 0)))
    end = cs[:, POOL_STATE + 1:]
    tok = p_ext[:, POOL_STATE:].astype(jnp.float32)
    pos = start_pos + jnp.arange(T)
    outs = []
    for gi, w in enumerate(POOL_WINDOWS):
        sl = slice(gi * POOL_GROUP, (gi + 1) * POOL_GROUP)
        begin = cs[:, POOL_STATE + 1 - w:POOL_STATE + 1 - w + T, sl]
        cnt = jnp.minimum(pos + 1, w).astype(jnp.float32)[None, :, None]
        outs.append((end[..., sl] - begin) / cnt - tok[..., sl])
    pooled = jnp.stack(outs, axis=2).astype(p_ext.dtype)
    y = jnp.einsum('btgc,gcd->btgd', pooled, w_pool).reshape(B, T, C)
    return y * pool_scale


def _spatial_gate(u, v, w_s, b_s):
    B, T, C = v.shape
    L = min(T, GMLP_CHUNK)
    n = T // L
    mask = jnp.tril(jnp.ones((L, L), dtype=bool))
    ws = jnp.where(mask[None], w_s[:, :L, :L], 0).astype(v.dtype)
    vb = v.reshape(B, n, L, N_GMLP_HEADS, GMLP_HEAD)
    mixed = jnp.einsum('hij,bnjhc->bnihc', ws, vb) + b_s[:, :L].T[None, None, :, :, None]
    return u * mixed.reshape(B, T, C)


def _layer(x, st_pool, st_sconv, st_cconv, start_pos, p):
    xn = _rmsnorm(x, p['g_mix'])
    z = jnp.einsum('btd,dk->btk', xn, p['w_in'])
    dt = z.dtype
    pool_ext = jnp.concatenate([st_pool.astype(dt), z[..., OFF_POOL:OFF_POOL + D_BRANCH]], axis=1)
    y_a = _pool_mixer(pool_ext, start_pos, p['w_pool'], p['pool_scale'])
    h = z[..., OFF_SC:OFF_SC + D_BRANCH]
    b_gate = z[..., OFF_SC + D_BRANCH:OFF_SC + 2 * D_BRANCH]
    c_gate = z[..., OFF_SC + 2 * D_BRANCH:OFF_SC + 3 * D_BRANCH]
    sc_ext = jnp.concatenate([st_sconv.astype(dt), c_gate * h], axis=1)
    y_b = b_gate * _dwconv_causal(sc_ext, p['w_sconv'])
    glu = z[..., OFF_CC:OFF_CC + D_BRANCH] * jax.nn.sigmoid(z[..., OFF_CC + D_BRANCH:OFF_CC + 2 * D_BRANCH])
    cc_ext = jnp.concatenate([st_cconv.astype(dt), glu], axis=1)
    y_c = jax.nn.silu(_layernorm(_dwconv_causal(cc_ext, p['w_cconv']) + p['b_cconv'], p['g_cln'], p['b_cln']))
    u = z[..., OFF_GM:OFF_GM + D_BRANCH]
    v = _layernorm(z[..., OFF_GM + D_BRANCH:OFF_GM + 2 * D_BRANCH], p['g_vln'], p['b_vln'])
    y_d = _spatial_gate(u, v, p['w_spatial'], p['b_spatial'])
    merged = None
    for i, yb in enumerate((y_a, y_b, y_c, y_d)):
        gate = jax.nn.sigmoid(z[..., OFF_GATE + i * D_MODEL:OFF_GATE + (i + 1) * D_MODEL])
        term = gate * jnp.einsum('btc,cd->btd', yb, p['w_branch'][i])
        merged = term if merged is None else merged + term
    x = x + jnp.einsum('btd,de->bte', merged, p['w_out'])
    hid = jnp.square(jax.nn.relu(jnp.einsum('btd,df->btf', _rmsnorm(x, p['g_ffn']), p['w_ff1'])))
    x = x + jnp.einsum('btf,fd->btd', hid, p['w_ff2'])
    return (x, pool_ext[:, -POOL_STATE:], sc_ext[:, -(SCONV_W - 1):], cc_ext[:, -(CCONV_W - 1):], v)


def _trunk(x, st_pool, st_sconv, st_cconv, start_pos, layers, g_final):
    pools, sconvs, cconvs, vs = [], [], [], []
    for l in range(DEPTH):
        x, sp, ss, sc, v = _layer(x, st_pool[l], st_sconv[l], st_cconv[l], start_pos, layers[l])
        pools.append(sp)
        sconvs.append(ss)
        cconvs.append(sc)
        vs.append(v)
    return (_rmsnorm(x, g_final), jnp.stack(pools), jnp.stack(sconvs), jnp.stack(cconvs), vs)


def setup_inputs(seed: int = 0) -> dict:
    key = jax.random.key(seed)
    ks = jax.random.split(key, 24)
    f32 = jnp.float32

    def nrm(k, shape, scale):
        return scale * jax.random.normal(k, shape, f32)

    return {
        'x_prompt': nrm(ks[0], (BATCH, SEQ, D_MODEL), 1.0),
        'x_sample': nrm(ks[1], (DEC_BATCH, DEC_SEQ, D_MODEL), 1.0),
        'state_pool': nrm(ks[2], (DEPTH, DEC_BATCH, POOL_STATE, D_BRANCH), 1.0),
        'state_sconv': nrm(ks[3], (DEPTH, DEC_BATCH, SCONV_W - 1, D_BRANCH), 1.0),
        'state_cconv': nrm(ks[4], (DEPTH, DEC_BATCH, CCONV_W - 1, D_BRANCH), 0.5),
        'g_mix': 1.0 + nrm(ks[5], (DEPTH, D_MODEL), 0.02),
        'w_in': nrm(ks[6], (DEPTH, D_MODEL, IN_COLS), D_MODEL ** -0.5),
        'w_pool': nrm(ks[7], (DEPTH, N_POOL_GROUPS, POOL_GROUP, POOL_GROUP), POOL_GROUP ** -0.5),
        'pool_scale': 1.0 + nrm(ks[8], (DEPTH, D_BRANCH), 0.02),
        'w_sconv': nrm(ks[9], (DEPTH, SCONV_W, D_BRANCH), SCONV_W ** -0.5),
        'w_cconv': nrm(ks[10], (DEPTH, CCONV_W, D_BRANCH), CCONV_W ** -0.5),
        'b_cconv': nrm(ks[11], (DEPTH, D_BRANCH), 0.01),
        'g_cln': 1.0 + nrm(ks[12], (DEPTH, D_BRANCH), 0.02),
        'b_cln': nrm(ks[13], (DEPTH, D_BRANCH), 0.01),
        'g_vln': 1.0 + nrm(ks[14], (DEPTH, D_BRANCH), 0.02),
        'b_vln': nrm(ks[15], (DEPTH, D_BRANCH), 0.01),
        'w_spatial': nrm(ks[16], (DEPTH, N_GMLP_HEADS, GMLP_CHUNK, GMLP_CHUNK), GMLP_CHUNK ** -0.5),
        'b_spatial': 1.0 + nrm(ks[17], (DEPTH, N_GMLP_HEADS, GMLP_CHUNK), 0.02),
        'w_branch': nrm(ks[18], (DEPTH, N_BRANCH, D_BRANCH, D_MODEL), D_BRANCH ** -0.5),
        'w_out': nrm(ks[19], (DEPTH, D_MODEL, D_MODEL), D_MODEL ** -0.5),
        'g_ffn': 1.0 + nrm(ks[20], (DEPTH, D_MODEL), 0.02),
        'w_ff1': nrm(ks[21], (DEPTH, D_MODEL, D_FF), D_MODEL ** -0.5),
        'w_ff2': nrm(ks[22], (DEPTH, D_FF, D_MODEL), D_FF ** -0.5),
        'g_final': 1.0 + nrm(ks[23], (D_MODEL,), 0.02),
    }


def reference(x_prompt, x_sample, state_pool, state_sconv, state_cconv, g_mix, w_in, w_pool,
              pool_scale, w_sconv, w_cconv, b_cconv, g_cln, b_cln, g_vln, b_vln, w_spatial,
              b_spatial, w_branch, w_out, g_ffn, w_ff1, w_ff2, g_final):
    assert x_sample.shape[1] <= CHUNK
    layers = [dict(g_mix=g_mix[l], w_in=w_in[l], w_pool=w_pool[l], pool_scale=pool_scale[l],
                   w_sconv=w_sconv[l], w_cconv=w_cconv[l], b_cconv=b_cconv[l], g_cln=g_cln[l],
                   b_cln=b_cln[l], g_vln=g_vln[l], b_vln=b_vln[l], w_spatial=w_spatial[l],
                   b_spatial=b_spatial[l], w_branch=w_branch[l], w_out=w_out[l], g_ffn=g_ffn[l],
                   w_ff1=w_ff1[l], w_ff2=w_ff2[l]) for l in range(DEPTH)]
    B = x_prompt.shape[0]
    dt = x_prompt.dtype
    zp = jnp.zeros((DEPTH, B, POOL_STATE, D_BRANCH), dt)
    zs = jnp.zeros((DEPTH, B, SCONV_W - 1, D_BRANCH), dt)
    zc = jnp.zeros((DEPTH, B, CCONV_W - 1, D_BRANCH), dt)
    y_prompt, pool_p, sconv_p, cconv_p, _ = _trunk(x_prompt, zp, zs, zc, 0, layers, g_final)
    y_sample, pool_s, sconv_s, cconv_s, vs = _trunk(
        x_sample, state_pool, state_sconv, state_cconv, PAST_LEN, layers, g_final)
    gmlp_v_s = jnp.stack(vs)
    return (y_prompt, y_sample, pool_p, pool_s, sconv_p, sconv_s, cconv_p, cconv_s, gmlp_v_s)
```

```python
import functools

import jax
import jax.numpy as jnp
from jax import lax
from jax.experimental import pallas as pl
from jax.experimental.pallas import tpu as pltpu

F32 = jnp.float32
BF16 = jnp.bfloat16

D_MODEL = 4096
DEPTH = 2
N_BRANCH = 4
D_BRANCH = D_MODEL // N_BRANCH
POOL_WINDOWS = (2, 4, 8, 16)
POOL_GROUP = D_BRANCH // len(POOL_WINDOWS)
POOL_STATE = max(POOL_WINDOWS) - 1
SCONV_W = 3
CCONV_W = 31
GMLP_CHUNK = 128
N_GMLP_HEADS = 8
GMLP_HEAD = D_BRANCH // N_GMLP_HEADS
D_FF = 4 * D_MODEL
EPS = 1e-6
OFF_POOL = 0
OFF_SC = OFF_POOL + D_BRANCH
OFF_CC = OFF_SC + 3 * D_BRANCH
OFF_GM = OFF_CC + 2 * D_BRANCH
OFF_GATE = OFF_GM + 2 * D_BRANCH
IN_COLS = OFF_GATE + N_BRANCH * D_MODEL
MIX_COLS = OFF_GATE

V7X_VMEM_BYTES = 64 * 1024 * 1024
V7X_SUBLANES_BF16 = 16
MiB = 1024 * 1024

HALO = 32
MIX_RC = 32


def _vmem_limit(nbytes):
    return int(min(V7X_VMEM_BYTES - 4 * MiB, nbytes + 12 * MiB))


def _row_tile(m, target):
    best = None
    for t in range(V7X_SUBLANES_BF16, target + 1, V7X_SUBLANES_BF16):
        if m % t == 0:
            best = t
    assert best is not None, (m, target)
    return best


def _rmsnorm_kernel(x_ref, g_ref, o_ref):
    x = x_ref[...]
    ms = jnp.mean(x * x, axis=-1, keepdims=True)
    o_ref[...] = (x * lax.rsqrt(ms + EPS) * g_ref[...]).astype(o_ref.dtype)


def _rmsnorm(x, g, *, out_dtype, rows=None, row_block_offset=0, tr):
    m, d = x.shape
    rows = m if rows is None else rows
    assert rows % tr == 0
    nbytes = 2 * tr * d * (4 + jnp.dtype(out_dtype).itemsize)
    return pl.pallas_call(
        _rmsnorm_kernel,
        grid=(rows // tr,),
        in_specs=[pl.BlockSpec((tr, d), lambda i: (i + row_block_offset, 0)),
                  pl.BlockSpec((1, d), lambda i: (0, 0))],
        out_specs=pl.BlockSpec((tr, d), lambda i: (i, 0)),
        out_shape=jax.ShapeDtypeStruct((rows, d), out_dtype),
        compiler_params=pltpu.CompilerParams(
            dimension_semantics=("arbitrary",), vmem_limit_bytes=_vmem_limit(nbytes)),
        name="rmsnorm",
    )(x, g.reshape(1, d))


def _mm_kernel(a_ref, w_ref, o_ref, *, act):
    acc = jnp.dot(a_ref[...], w_ref[...], preferred_element_type=F32)
    if act == "relu2":
        acc = jnp.square(jnp.maximum(acc, 0.0))
    o_ref[...] = acc.astype(o_ref.dtype)


def _mm_res_kernel(a_ref, w_ref, r_ref, o_ref):
    acc = jnp.dot(a_ref[...], w_ref[...], preferred_element_type=F32)
    o_ref[...] = r_ref[...] + acc


def _matmul(a, w, *, tm, tn, out_dtype, act=None, res=None, name):
    m, k = a.shape
    k2, n = w.shape
    assert k == k2 and m % tm == 0 and n % tn == 0
    osz = jnp.dtype(out_dtype).itemsize
    nbytes = 2 * (tm * k * 2 + k * tn * 2 + tm * tn * osz) + tm * tn * 4
    in_specs = [pl.BlockSpec((tm, k), lambda i, j: (i, 0)),
                pl.BlockSpec((k, tn), lambda i, j: (0, j))]
    args = [a, w]
    if res is None:
        body = functools.partial(_mm_kernel, act=act)
    else:
        body = _mm_res_kernel
        in_specs.append(pl.BlockSpec((tm, tn), lambda i, j: (i, j)))
        args.append(res)
        nbytes += 2 * tm * tn * 4
    return pl.pallas_call(
        body,
        grid=(m // tm, n // tn),
        in_specs=in_specs,
        out_specs=pl.BlockSpec((tm, tn), lambda i, j: (i, j)),
        out_shape=jax.ShapeDtypeStruct((m, n), out_dtype),
        compiler_params=pltpu.CompilerParams(
            dimension_semantics=("arbitrary", "arbitrary"), vmem_limit_bytes=_vmem_limit(nbytes)),
        name=name,
    )(*args)


def _mm_kacc_kernel(a_ref, w_ref, r_ref, o_ref):
    @pl.when(pl.program_id(2) == 0)
    def _():
        o_ref[...] = r_ref[...]

    o_ref[...] += jnp.dot(a_ref[...], w_ref[...], preferred_element_type=F32)


def _matmul_kacc(a, w, res, *, tm, tn, tk, name):
    m, k = a.shape
    k2, n = w.shape
    assert k == k2 and m % tm == 0 and n % tn == 0 and k % tk == 0
    nbytes = 2 * (tm * tk * 2 + tk * tn * 2 + 2 * tm * tn * 4) + tm * tn * 4
    return pl.pallas_call(
        _mm_kacc_kernel,
        grid=(m // tm, n // tn, k // tk),
        in_specs=[pl.BlockSpec((tm, tk), lambda i, j, kk: (i, kk)),
                  pl.BlockSpec((tk, tn), lambda i, j, kk: (kk, j)),
                  pl.BlockSpec((tm, tn), lambda i, j, kk: (i, j))],
        out_specs=pl.BlockSpec((tm, tn), lambda i, j, kk: (i, j)),
        out_shape=jax.ShapeDtypeStruct((m, n), F32),
        compiler_params=pltpu.CompilerParams(
            dimension_semantics=("arbitrary", "arbitrary", "arbitrary"),
            vmem_limit_bytes=_vmem_limit(nbytes)),
        name=name,
    )(a, w, res)


def _branch_kernel(y_ref, g0_ref, g1_ref, g2_ref, g3_ref, wb_ref, o_ref):
    acc = None
    for i, g_ref in enumerate((g0_ref, g1_ref, g2_ref, g3_ref)):
        d = jnp.dot(y_ref[:, i * D_BRANCH:(i + 1) * D_BRANCH], wb_ref[i],
                    preferred_element_type=F32)
        t = jax.nn.sigmoid(g_ref[...].astype(F32)) * d
        acc = t if acc is None else acc + t
    o_ref[...] = acc.astype(o_ref.dtype)


def _branch_merge(y, z, wb, *, tm, tn):
    m = y.shape[0]
    assert m % tm == 0 and D_MODEL % tn == 0 and OFF_GATE % tn == 0
    nbytes = 2 * (tm * D_MODEL * 2 + N_BRANCH * tm * tn * 2 + N_BRANCH * D_BRANCH * tn * 2
                  + tm * tn * 2) + 3 * tm * tn * 4

    def gate_spec(i):
        base = (OFF_GATE + i * D_MODEL) // tn
        return pl.BlockSpec((tm, tn), lambda r, j: (r, base + j))

    return pl.pallas_call(
        _branch_kernel,
        grid=(m // tm, D_MODEL // tn),
        in_specs=[pl.BlockSpec((tm, D_MODEL), lambda r, j: (r, 0))]
        + [gate_spec(i) for i in range(N_BRANCH)]
        + [pl.BlockSpec((N_BRANCH, D_BRANCH, tn), lambda r, j: (0, 0, j))],
        out_specs=pl.BlockSpec((tm, tn), lambda r, j: (r, j)),
        out_shape=jax.ShapeDtypeStruct((m, D_MODEL), BF16),
        compiler_params=pltpu.CompilerParams(
            dimension_semantics=("arbitrary", "arbitrary"), vmem_limit_bytes=_vmem_limit(nbytes)),
        name="branch_merge",
    )(y, z, z, z, z, wb)


def _layernorm_rows(x, g, b):
    mu = jnp.mean(x, axis=-1, keepdims=True)
    xc = x - mu
    var = jnp.mean(xc * xc, axis=-1, keepdims=True)
    return xc * lax.rsqrt(var + EPS) * g + b


def _mixer_kernel(*refs, tt, chunk, start_pos, has_state, has_prev, emit_v):
    refs = list(refs)
    z_ref = refs.pop(0)
    if has_state:
        stp_ref, sts_ref, stc_ref = refs.pop(0), refs.pop(0), refs.pop(0)
    if has_prev:
        refs.pop(0)
    (wpool_ref, pscale_ref, wsc_ref, wcc_ref, bcc_ref, gcln_ref, bcln_ref, gvln_ref, bvln_ref,
     ws_ref, bs_ref) = refs[:11]
    refs = refs[11:]
    y_ref, npool_ref, nsc_ref, ncc_ref = refs[:4]
    refs = refs[4:]
    if emit_v:
        v_ref = refs.pop(0)
    pext, scext, ccext, pooled_s, vb_s = refs

    s = pl.program_id(1)
    C = D_BRANCH
    rc = min(MIX_RC, tt)

    @pl.when(s == 0)
    def _():
        if has_state:
            pext[0:HALO, :] = stp_ref[0]
            scext[0:HALO, :] = sts_ref[0]
            ccext[0:HALO, :] = stc_ref[0]
        else:
            zero = jnp.zeros((HALO, C), F32)
            pext[0:HALO, :] = zero
            scext[0:HALO, :] = zero
            ccext[0:HALO, :] = zero

    for r in range(0, tt, rc):
        rows = pl.ds(r, rc)
        dst = pl.ds(HALO + r, rc)
        pext[dst, :] = z_ref[rows, OFF_POOL:OFF_POOL + C].astype(F32)
        h = z_ref[rows, OFF_SC:OFF_SC + C].astype(F32)
        cg = z_ref[rows, OFF_SC + 2 * C:OFF_SC + 3 * C].astype(F32)
        scext[dst, :] = cg * h
        a = z_ref[rows, OFF_CC:OFF_CC + C].astype(F32)
        g = z_ref[rows, OFF_CC + C:OFF_CC + 2 * C].astype(F32)
        ccext[dst, :] = a * jax.nn.sigmoid(g)

    pos0 = start_pos + s * tt
    for r in range(0, tt, rc):
        pos = pos0 + r + lax.broadcasted_iota(jnp.int32, (rc, POOL_GROUP), 0)
        for gi, w in enumerate(POOL_WINDOWS):
            lanes = slice(gi * POOL_GROUP, (gi + 1) * POOL_GROUP)
            tok = pext[pl.ds(HALO + r, rc), lanes]
            tot = tok
            for k in range(1, w):
                tot = tot + pext[pl.ds(HALO + r - k, rc), lanes]
            cnt = jnp.minimum(pos + 1, w).astype(F32)
            pooled_s[pl.ds(r, rc), lanes] = (tot / cnt - tok).astype(BF16)
    for gi in range(len(POOL_WINDOWS)):
        lanes = slice(gi * POOL_GROUP, (gi + 1) * POOL_GROUP)
        ya = jnp.dot(pooled_s[:, lanes], wpool_ref[gi], preferred_element_type=F32)
        y_ref[:, lanes] = (ya * pscale_ref[:, lanes]).astype(BF16)

    for r in range(0, tt, rc):
        rows = pl.ds(r, rc)
        acc = None
        for k in range(SCONV_W):
            t = wsc_ref[k:k + 1, :] * scext[pl.ds(HALO + r - (SCONV_W - 1) + k, rc), :]
            acc = t if acc is None else acc + t
        bg = z_ref[rows, OFF_SC + C:OFF_SC + 2 * C].astype(F32)
        y_ref[rows, C:2 * C] = (bg * acc).astype(BF16)

    for r in range(0, tt, rc):
        acc = None
        for k in range(CCONV_W):
            t = wcc_ref[k:k + 1, :] * ccext[pl.ds(HALO + r - (CCONV_W - 1) + k, rc), :]
            acc = t if acc is None else acc + t
        yl = _layernorm_rows(acc + bcc_ref[...], gcln_ref[...], bcln_ref[...])
        y_ref[pl.ds(r, rc), 2 * C:3 * C] = (yl * jax.nn.sigmoid(yl)).astype(BF16)

    for r in range(0, tt, rc):
        rows = pl.ds(r, rc)
        v = _layernorm_rows(z_ref[rows, OFF_GM + C:OFF_GM + 2 * C].astype(F32),
                            gvln_ref[...], bvln_ref[...])
        if emit_v:
            v_ref[rows, :] = v
        vb_s[rows, :] = v.astype(BF16)
    ri = lax.broadcasted_iota(jnp.int32, (chunk, chunk), 0)
    ci = lax.broadcasted_iota(jnp.int32, (chunk, chunk), 1)
    for hh in range(N_GMLP_HEADS):
        lanes = slice(hh * GMLP_HEAD, (hh + 1) * GMLP_HEAD)
        wsm = jnp.where(ci <= ri, ws_ref[hh], 0.0).astype(BF16)
        for n in range(tt // chunk):
            rows = pl.ds(n * chunk, chunk)
            mixed = jnp.dot(wsm, vb_s[rows, lanes], preferred_element_type=F32) + bs_ref[:, lanes]
            u = z_ref[rows, OFF_GM + hh * GMLP_HEAD:OFF_GM + (hh + 1) * GMLP_HEAD].astype(F32)
            y_ref[rows, 3 * C + hh * GMLP_HEAD:3 * C + (hh + 1) * GMLP_HEAD] = (u * mixed).astype(BF16)

    for ext, out in ((pext, npool_ref), (scext, nsc_ref), (ccext, ncc_ref)):
        tail = ext[tt:tt + HALO, :]
        ext[0:HALO, :] = tail
        out[0] = tail


def _mixers(z, *, n_seq, t, tt, row_offset, start_pos, states, y_prev, lw, emit_v):
    m = z.shape[0]
    C = D_BRANCH
    assert t % tt == 0 and tt >= HALO and row_offset % tt == 0
    chunk = min(t, GMLP_CHUNK)
    assert tt % chunk == 0
    nt = t // tt
    blk0 = row_offset // tt
    has_state = states is not None
    has_prev = y_prev is not None

    def row_map(b, s):
        return (blk0 + b * nt + s, 0)

    def const2(b, s):
        return (0, 0)

    def const3(b, s):
        return (0, 0, 0)

    def seq3(b, s):
        return (b, 0, 0)

    in_specs = [pl.BlockSpec((tt, MIX_COLS), row_map)]
    args = [z]
    if has_state:
        for st in states:
            in_specs.append(pl.BlockSpec((1, HALO, C), seq3))
            args.append(st)
    if has_prev:
        in_specs.append(pl.BlockSpec(memory_space=pl.ANY))
        args.append(y_prev)
    ws = lw["w_spatial"][:, :chunk, :chunk]
    bs = jnp.repeat(lw["b_spatial"][:, :chunk].T, GMLP_HEAD, axis=1)
    small = [
        (lw["w_pool"].astype(BF16), const3), (lw["pool_scale"].reshape(1, C), const2),
        (lw["w_sconv"], const2), (lw["w_cconv"], const2), (lw["b_cconv"].reshape(1, C), const2),
        (lw["g_cln"].reshape(1, C), const2), (lw["b_cln"].reshape(1, C), const2),
        (lw["g_vln"].reshape(1, C), const2), (lw["b_vln"].reshape(1, C), const2),
        (ws, const3), (bs, const2),
    ]
    for arr, imap in small:
        in_specs.append(pl.BlockSpec(arr.shape, imap))
        args.append(arr)

    out_shape = [jax.ShapeDtypeStruct((m, D_MODEL), BF16)] + [
        jax.ShapeDtypeStruct((n_seq, HALO, C), F32)] * 3
    out_specs = [pl.BlockSpec((tt, D_MODEL), row_map)] + [pl.BlockSpec((1, HALO, C), seq3)] * 3
    if emit_v:
        out_shape.append(jax.ShapeDtypeStruct((n_seq * t, C), F32))
        out_specs.append(pl.BlockSpec((tt, C), lambda b, s: (b * nt + s, 0)))

    aliases = {}
    if has_prev:
        aliases = {1 + (3 if has_state else 0): 0}
    nbytes = (2 * (tt * MIX_COLS * 2 + tt * D_MODEL * 2 + tt * C * 4)
              + 3 * (tt + HALO) * C * 4 + 2 * tt * C * 2 + 16 * HALO * C * 4
              + 16 * tt * C * 4)
    kern = functools.partial(_mixer_kernel, tt=tt, chunk=chunk, start_pos=start_pos,
                             has_state=has_state, has_prev=has_prev, emit_v=emit_v)
    return pl.pallas_call(
        kern,
        grid=(n_seq, nt),
        in_specs=in_specs,
        out_specs=out_specs,
        out_shape=out_shape,
        scratch_shapes=[pltpu.VMEM((HALO + tt, C), F32)] * 3 + [pltpu.VMEM((tt, C), BF16)] * 2,
        input_output_aliases=aliases,
        compiler_params=pltpu.CompilerParams(
            dimension_semantics=("arbitrary", "arbitrary"), vmem_limit_bytes=_vmem_limit(nbytes)),
        name="mixers_state" if has_state else "mixers",
    )(*args)


def _pad_state(st):
    return jnp.pad(st, ((0, 0), (HALO - st.shape[1], 0), (0, 0)))


def kernel(x_prompt, x_sample, state_pool, state_sconv, state_cconv, g_mix, w_in, w_pool,
           pool_scale, w_sconv, w_cconv, b_cconv, g_cln, b_cln, g_vln, b_vln, w_spatial,
           b_spatial, w_branch, w_out, g_ffn, w_ff1, w_ff2, g_final):
    bp, tp, d = x_prompt.shape
    bs_, ts, _ = x_sample.shape
    assert d == D_MODEL and ts <= 64
    mp, ms = bp * tp, bs_ * ts
    m = mp + ms
    past_len = 2048

    tm = _row_tile(m, 1100)
    tr = _row_tile(m, 400)
    tt_p = 256
    tt_s = ts

    x = jnp.concatenate([x_prompt.reshape(mp, d), x_sample.reshape(ms, d)], axis=0)

    new_pool_p, new_pool_s, new_sc_p, new_sc_s, new_cc_p, new_cc_s, vs = [], [], [], [], [], [], []
    for l in range(DEPTH):
        lw = dict(w_pool=w_pool[l], pool_scale=pool_scale[l], w_sconv=w_sconv[l],
                  w_cconv=w_cconv[l], b_cconv=b_cconv[l], g_cln=g_cln[l], b_cln=b_cln[l],
                  g_vln=g_vln[l], b_vln=b_vln[l], w_spatial=w_spatial[l], b_spatial=b_spatial[l])
        xn = _rmsnorm(x, g_mix[l], out_dtype=BF16, tr=tr)
        z = _matmul(xn, w_in[l].astype(BF16), tm=tm, tn=1024, out_dtype=BF16, name="in_proj")

        y, npp, nsp, ncp = _mixers(z, n_seq=bp, t=tp, tt=tt_p, row_offset=0, start_pos=0,
                                   states=None, y_prev=None, lw=lw, emit_v=False)
        states = (_pad_state(state_pool[l]), _pad_state(state_sconv[l]), _pad_state(state_cconv[l]))
        y, nps, nss, ncs, v = _mixers(z, n_seq=bs_, t=ts, tt=tt_s, row_offset=mp,
                                      start_pos=past_len, states=states, y_prev=y, lw=lw,
                                      emit_v=True)
        new_pool_p.append(npp[:, HALO - POOL_STATE:])
        new_pool_s.append(nps[:, HALO - POOL_STATE:])
        new_sc_p.append(nsp[:, HALO - (SCONV_W - 1):])
        new_sc_s.append(nss[:, HALO - (SCONV_W - 1):])
        new_cc_p.append(ncp[:, HALO - (CCONV_W - 1):])
        new_cc_s.append(ncs[:, HALO - (CCONV_W - 1):])
        vs.append(v.reshape(bs_, ts, D_BRANCH))

        merged = _branch_merge(y, z, w_branch[l].astype(BF16), tm=tm, tn=512)
        x = _matmul(merged, w_out[l].astype(BF16), tm=tm, tn=1024, out_dtype=F32, res=x,
                    name="out_proj")
        xn = _rmsnorm(x, g_ffn[l], out_dtype=BF16, tr=tr)
        hid = _matmul(xn, w_ff1[l].astype(BF16), tm=tm, tn=1024, out_dtype=BF16, act="relu2",
                      name="ff1")
        x = _matmul_kacc(hid, w_ff2[l].astype(BF16), x, tm=tm, tn=1024, tk=2048, name="ff2")

    tf = _row_tile(ms, 512)
    assert mp % tf == 0
    y_prompt = _rmsnorm(x, g_final, out_dtype=F32, rows=mp, row_block_offset=0, tr=tf)
    y_sample = _rmsnorm(x, g_final, out_dtype=F32, rows=ms, row_block_offset=mp // tf, tr=tf)
    return (y_prompt.reshape(bp, tp, d), y_sample.reshape(bs_, ts, d),
            jnp.stack(new_pool_p), jnp.stack(new_pool_s),
            jnp.stack(new_sc_p), jnp.stack(new_sc_s),
            jnp.stack(new_cc_p), jnp.stack(new_cc_s),
            jnp.stack(vs))
```

```python
import functools

import jax
import jax.numpy as jnp
from jax import lax
from jax.experimental import pallas as pl
from jax.experimental.pallas import tpu as pltpu

F32 = jnp.float32
BF16 = jnp.bfloat16

D_MODEL = 4096
DEPTH = 2
PAST_LEN = 2048
N_BRANCH = 4
D_BRANCH = D_MODEL // N_BRANCH
POOL_WINDOWS = (2, 4, 8, 16)
POOL_GROUP = D_BRANCH // len(POOL_WINDOWS)
POOL_STATE = max(POOL_WINDOWS) - 1
SCONV_W = 3
CCONV_W = 31
GMLP_CHUNK = 128
N_GMLP_HEADS = 8
GMLP_HEAD = D_BRANCH // N_GMLP_HEADS
D_FF = 4 * D_MODEL
EPS = 1e-6
OFF_POOL = 0
OFF_SC = OFF_POOL + D_BRANCH
OFF_CC = OFF_SC + 3 * D_BRANCH
OFF_GM = OFF_CC + 2 * D_BRANCH
OFF_GATE = OFF_GM + 2 * D_BRANCH
IN_COLS = OFF_GATE + N_BRANCH * D_MODEL
MIX_COLS = OFF_GATE

V7X_VMEM_BYTES = 64 * 1024 * 1024
V7X_LANES = 128
V7X_SUBLANES = 8
V7X_SUBLANES_BF16 = 16
MiB = 1024 * 1024

HALO = 32
MIX_RC = 32


def _vmem_limit(nbytes):
    return int(min(V7X_VMEM_BYTES - 4 * MiB, nbytes + 12 * MiB))


def _row_tile(m, target):
    best = None
    for t in range(V7X_SUBLANES_BF16, target + 1, V7X_SUBLANES_BF16):
        if m % t == 0:
            best = t
    assert best is not None, (m, target)
    return best


def _fold_lanes(v):
    acc = v[:, 0:V7X_LANES]
    for c in range(1, v.shape[1] // V7X_LANES):
        acc = acc + v[:, c * V7X_LANES:(c + 1) * V7X_LANES]
    return acc


def _row_scale(ss_ref):
    return lax.rsqrt(jnp.sum(ss_ref[...], axis=-1, keepdims=True) * (1.0 / D_MODEL) + EPS)


def _emit_stream(x, g_ref, xb_ref, ss_ref, first):
    xb_ref[...] = (x * g_ref[...]).astype(BF16)
    part = _fold_lanes(x * x)

    @pl.when(first)
    def _():
        ss_ref[...] = part

    @pl.when(jnp.logical_not(first))
    def _():
        ss_ref[...] += part


def _prep_kernel(xp_ref, xs_ref, g_ref, x_ref, xb_ref, ss_ref, *, n_prompt_tiles):
    def emit(src_ref):
        x = src_ref[...]
        x_ref[...] = x
        xb_ref[...] = (x * g_ref[...]).astype(BF16)
        ss_ref[...] = _fold_lanes(x * x)

    i = pl.program_id(0)

    @pl.when(i < n_prompt_tiles)
    def _():
        emit(xp_ref)

    @pl.when(i >= n_prompt_tiles)
    def _():
        emit(xs_ref)


def _prep(xp, xs, g, *, tr):
    mp, d = xp.shape
    ms = xs.shape[0]
    assert mp % tr == 0 and ms % tr == 0
    npt = mp // tr
    m = mp + ms
    nbytes = 2 * tr * d * (4 + 4 + 4 + 2)
    return pl.pallas_call(
        functools.partial(_prep_kernel, n_prompt_tiles=npt),
        grid=(m // tr,),
        in_specs=[pl.BlockSpec((tr, d), lambda i: (jnp.minimum(i, npt - 1), 0)),
                  pl.BlockSpec((tr, d), lambda i: (jnp.maximum(i - npt, 0), 0)),
                  pl.BlockSpec((1, d), lambda i: (0, 0))],
        out_specs=[pl.BlockSpec((tr, d), lambda i: (i, 0)),
                   pl.BlockSpec((tr, d), lambda i: (i, 0)),
                   pl.BlockSpec((tr, V7X_LANES), lambda i: (i, 0))],
        out_shape=[jax.ShapeDtypeStruct((m, d), F32), jax.ShapeDtypeStruct((m, d), BF16),
                   jax.ShapeDtypeStruct((m, V7X_LANES), F32)],
        compiler_params=pltpu.CompilerParams(
            dimension_semantics=("arbitrary",), vmem_limit_bytes=_vmem_limit(nbytes)),
        name="prep",
    )(xp, xs, g.reshape(1, d))


def _rmsnorm_kernel(x_ref, g_ref, o_ref):
    x = x_ref[...]
    ms = jnp.mean(x * x, axis=-1, keepdims=True)
    o_ref[...] = (x * lax.rsqrt(ms + EPS) * g_ref[...]).astype(o_ref.dtype)


def _rmsnorm(x, g, *, out_dtype, rows, row_block_offset, tr):
    d = x.shape[1]
    assert rows % tr == 0
    nbytes = 2 * tr * d * (4 + jnp.dtype(out_dtype).itemsize)
    return pl.pallas_call(
        _rmsnorm_kernel,
        grid=(rows // tr,),
        in_specs=[pl.BlockSpec((tr, d), lambda i: (i + row_block_offset, 0)),
                  pl.BlockSpec((1, d), lambda i: (0, 0))],
        out_specs=pl.BlockSpec((tr, d), lambda i: (i, 0)),
        out_shape=jax.ShapeDtypeStruct((rows, d), out_dtype),
        compiler_params=pltpu.CompilerParams(
            dimension_semantics=("arbitrary",), vmem_limit_bytes=_vmem_limit(nbytes)),
        name="rmsnorm",
    )(x, g.reshape(1, d))


def _mm_norm_kernel(a_ref, w_ref, ss_ref, o_ref, *, act):
    acc = jnp.dot(a_ref[...], w_ref[...], preferred_element_type=F32)
    acc = acc * _row_scale(ss_ref)
    if act == "relu2":
        acc = jnp.square(jnp.maximum(acc, 0.0))
    o_ref[...] = acc.astype(o_ref.dtype)


def _matmul_norm(a, w, layer, ss, *, tm, tn, act=None, name):
    m, k = a.shape
    _, k2, n = w.shape
    assert k == k2 and m % tm == 0 and n % tn == 0
    nbytes = 2 * (tm * k * 2 + k * tn * 2 + tm * tn * 2 + tm * V7X_LANES * 4) + tm * tn * 4
    return pl.pallas_call(
        functools.partial(_mm_norm_kernel, act=act),
        grid=(m // tm, n // tn),
        in_specs=[pl.BlockSpec((tm, k), lambda i, j: (i, 0)),
                  pl.BlockSpec((None, k, tn), lambda i, j: (layer, 0, j)),
                  pl.BlockSpec((tm, V7X_LANES), lambda i, j: (i, 0))],
        out_specs=pl.BlockSpec((tm, tn), lambda i, j: (i, j)),
        out_shape=jax.ShapeDtypeStruct((m, n), BF16),
        compiler_params=pltpu.CompilerParams(
            dimension_semantics=("arbitrary", "arbitrary"), vmem_limit_bytes=_vmem_limit(nbytes)),
        name=name,
    )(a, w, ss)


def _mm_res_emit_kernel(a_ref, w_ref, r_ref, g_ref, o_ref, xb_ref, ss_ref):
    x = r_ref[...] + jnp.dot(a_ref[...], w_ref[...], preferred_element_type=F32)
    o_ref[...] = x
    _emit_stream(x, g_ref, xb_ref, ss_ref, pl.program_id(1) == 0)


def _matmul_res_emit(a, w, layer, res, g, *, tm, tn, name):
    m, k = a.shape
    _, k2, n = w.shape
    assert k == k2 and m % tm == 0 and n % tn == 0
    nbytes = 2 * (tm * k * 2 + k * tn * 2 + tm * tn * (4 + 4 + 2) + tm * V7X_LANES * 4) + 2 * tm * tn * 4
    return pl.pallas_call(
        _mm_res_emit_kernel,
        grid=(m // tm, n // tn),
        in_specs=[pl.BlockSpec((tm, k), lambda i, j: (i, 0)),
                  pl.BlockSpec((None, k, tn), lambda i, j: (layer, 0, j)),
                  pl.BlockSpec((tm, tn), lambda i, j: (i, j)),
                  pl.BlockSpec((1, tn), lambda i, j: (0, j))],
        out_specs=[pl.BlockSpec((tm, tn), lambda i, j: (i, j)),
                   pl.BlockSpec((tm, tn), lambda i, j: (i, j)),
                   pl.BlockSpec((tm, V7X_LANES), lambda i, j: (i, 0))],
        out_shape=[jax.ShapeDtypeStruct((m, n), F32), jax.ShapeDtypeStruct((m, n), BF16),
                   jax.ShapeDtypeStruct((m, V7X_LANES), F32)],
        compiler_params=pltpu.CompilerParams(
            dimension_semantics=("arbitrary", "arbitrary"), vmem_limit_bytes=_vmem_limit(nbytes)),
        name=name,
    )(a, w, res, g.reshape(1, n))


def _mm_kacc_kernel(a_ref, w_ref, r_ref, *rest, emit):
    if emit:
        g_ref, o_ref, xb_ref, ss_ref = rest
    else:
        (o_ref,) = rest
    kk = pl.program_id(2)
    d = jnp.dot(a_ref[...], w_ref[...], preferred_element_type=F32)

    @pl.when(kk == 0)
    def _():
        o_ref[...] = r_ref[...] + d

    @pl.when(kk > 0)
    def _():
        o_ref[...] += d

    if emit:
        @pl.when(kk == pl.num_programs(2) - 1)
        def _():
            _emit_stream(o_ref[...], g_ref, xb_ref, ss_ref, pl.program_id(1) == 0)


def _matmul_kacc(a, w, layer, res, g, *, tm, tn, tk, name):
    m, k = a.shape
    _, k2, n = w.shape
    assert k == k2 and m % tm == 0 and n % tn == 0 and k % tk == 0
    emit = g is not None
    nbytes = 2 * (tm * tk * 2 + tk * tn * 2 + 2 * tm * tn * 4) + 2 * tm * tn * 4
    in_specs = [pl.BlockSpec((tm, tk), lambda i, j, kk: (i, kk)),
                pl.BlockSpec((None, tk, tn), lambda i, j, kk: (layer, kk, j)),
                pl.BlockSpec((tm, tn), lambda i, j, kk: (i, j))]
    args = [a, w, res]
    out_specs = [pl.BlockSpec((tm, tn), lambda i, j, kk: (i, j))]
    out_shape = [jax.ShapeDtypeStruct((m, n), F32)]
    if emit:
        in_specs.append(pl.BlockSpec((1, tn), lambda i, j, kk: (0, j)))
        args.append(g.reshape(1, n))
        out_specs += [pl.BlockSpec((tm, tn), lambda i, j, kk: (i, j)),
                      pl.BlockSpec((tm, V7X_LANES), lambda i, j, kk: (i, 0))]
        out_shape += [jax.ShapeDtypeStruct((m, n), BF16), jax.ShapeDtypeStruct((m, V7X_LANES), F32)]
        nbytes += 2 * (tm * tn * 2 + tm * V7X_LANES * 4)
    out = pl.pallas_call(
        functools.partial(_mm_kacc_kernel, emit=emit),
        grid=(m // tm, n // tn, k // tk),
        in_specs=in_specs,
        out_specs=out_specs,
        out_shape=out_shape,
        compiler_params=pltpu.CompilerParams(
            dimension_semantics=("arbitrary", "arbitrary", "arbitrary"),
            vmem_limit_bytes=_vmem_limit(nbytes)),
        name=name,
    )(*args)
    return out if emit else out[0]


def _branch_kernel(y_ref, g0_ref, g1_ref, g2_ref, g3_ref, wb_ref, o_ref):
    acc = None
    for i, g_ref in enumerate((g0_ref, g1_ref, g2_ref, g3_ref)):
        d = jnp.dot(y_ref[:, i * D_BRANCH:(i + 1) * D_BRANCH], wb_ref[i],
                    preferred_element_type=F32)
        t = jax.nn.sigmoid(g_ref[...].astype(F32)) * d
        acc = t if acc is None else acc + t
    o_ref[...] = acc.astype(o_ref.dtype)


def _branch_merge(y, z, wb, layer, *, tm, tn):
    m = y.shape[0]
    assert m % tm == 0 and D_MODEL % tn == 0 and OFF_GATE % tn == 0
    nbytes = 2 * (tm * D_MODEL * 2 + N_BRANCH * tm * tn * 2 + N_BRANCH * D_BRANCH * tn * 2
                  + tm * tn * 2) + 3 * tm * tn * 4

    def gate_spec(i):
        base = (OFF_GATE + i * D_MODEL) // tn
        return pl.BlockSpec((tm, tn), lambda r, j: (r, base + j))

    return pl.pallas_call(
        _branch_kernel,
        grid=(m // tm, D_MODEL // tn),
        in_specs=[pl.BlockSpec((tm, D_MODEL), lambda r, j: (r, 0))]
        + [gate_spec(i) for i in range(N_BRANCH)]
        + [pl.BlockSpec((None, N_BRANCH, D_BRANCH, tn), lambda r, j: (layer, 0, 0, j))],
        out_specs=pl.BlockSpec((tm, tn), lambda r, j: (r, j)),
        out_shape=jax.ShapeDtypeStruct((m, D_MODEL), BF16),
        compiler_params=pltpu.CompilerParams(
            dimension_semantics=("arbitrary", "arbitrary"), vmem_limit_bytes=_vmem_limit(nbytes)),
        name="branch_merge",
    )(y, z, z, z, z, wb)


def _layernorm_rows(x, g, b):
    mu = jnp.mean(x, axis=-1, keepdims=True)
    xc = x - mu
    var = jnp.mean(xc * xc, axis=-1, keepdims=True)
    return xc * lax.rsqrt(var + EPS) * g + b


def _mixer_kernel(*refs, tt, chunk, start_pos, has_state, has_prev, emit_v):
    refs = list(refs)
    z_ref = refs.pop(0)
    if has_state:
        stp_ref, sts_ref, stc_ref = refs.pop(0), refs.pop(0), refs.pop(0)
    if has_prev:
        refs.pop(0)
    (wpool_ref, pscale_ref, wsc_ref, wcc_ref, bcc_ref, gcln_ref, bcln_ref, gvln_ref, bvln_ref,
     ws_ref, bs_ref) = refs[:11]
    refs = refs[11:]
    y_ref, npool_ref, nsc_ref, ncc_ref = refs[:4]
    refs = refs[4:]
    if emit_v:
        v_ref = refs.pop(0)
    pext, scext, ccext, ccsh, pooled_s, vb_s = refs

    s = pl.program_id(1)
    C = D_BRANCH
    rc = min(MIX_RC, tt)

    @pl.when(s == 0)
    def _():
        if has_state:
            pext[0:HALO, :] = stp_ref[0]
            scext[0:HALO, :] = sts_ref[0]
            ccext[0:HALO, :] = stc_ref[0]
        else:
            zero = jnp.zeros((HALO, C), F32)
            pext[0:HALO, :] = zero
            scext[0:HALO, :] = zero
            ccext[0:HALO, :] = zero

    for r in range(0, tt, rc):
        rows = pl.ds(r, rc)
        dst = pl.ds(HALO + r, rc)
        pext[dst, :] = z_ref[rows, OFF_POOL:OFF_POOL + C].astype(F32)
        h = z_ref[rows, OFF_SC:OFF_SC + C].astype(F32)
        cg = z_ref[rows, OFF_SC + 2 * C:OFF_SC + 3 * C].astype(F32)
        scext[dst, :] = cg * h
        a = z_ref[rows, OFF_CC:OFF_CC + C].astype(F32)
        g = z_ref[rows, OFF_CC + C:OFF_CC + 2 * C].astype(F32)
        ccext[dst, :] = a * jax.nn.sigmoid(g)

    pos0 = start_pos + s * tt
    for r in range(0, tt, rc):
        pos = pos0 + r + lax.broadcasted_iota(jnp.int32, (rc, POOL_GROUP), 0)
        for gi, w in enumerate(POOL_WINDOWS):
            lanes = slice(gi * POOL_GROUP, (gi + 1) * POOL_GROUP)
            tok = pext[pl.ds(HALO + r, rc), lanes]
            tot = tok
            for k in range(1, w):
                tot = tot + pext[pl.ds(HALO + r - k, rc), lanes]
            cnt = jnp.minimum(pos + 1, w).astype(F32)
            pooled_s[pl.ds(r, rc), lanes] = (tot / cnt - tok).astype(BF16)
    for gi in range(len(POOL_WINDOWS)):
        lanes = slice(gi * POOL_GROUP, (gi + 1) * POOL_GROUP)
        ya = jnp.dot(pooled_s[:, lanes], wpool_ref[gi], preferred_element_type=F32)
        y_ref[:, lanes] = (ya * pscale_ref[:, lanes]).astype(BF16)

    for r in range(0, tt, rc):
        rows = pl.ds(r, rc)
        acc = None
        for k in range(SCONV_W):
            t = wsc_ref[k:k + 1, :] * scext[pl.ds(HALO + r - (SCONV_W - 1) + k, rc), :]
            acc = t if acc is None else acc + t
        bg = z_ref[rows, OFF_SC + C:OFF_SC + 2 * C].astype(F32)
        y_ref[rows, C:2 * C] = (bg * acc).astype(BF16)

    sh_rows = ccsh.shape[1]
    for b in range(1, V7X_SUBLANES):
        for r in range(0, sh_rows, rc):
            n = min(rc, sh_rows - r)
            ccsh[b - 1, pl.ds(r, n), :] = ccext[pl.ds(r + b, n), :]
    for r in range(0, tt, rc):
        acc = None
        for k in range(CCONV_W):
            o = k + HALO - (CCONV_W - 1)
            o8, ob = (o // V7X_SUBLANES) * V7X_SUBLANES, o % V7X_SUBLANES
            src = ccext[pl.ds(r + o8, rc), :] if ob == 0 else ccsh[ob - 1, pl.ds(r + o8, rc), :]
            t = wcc_ref[k:k + 1, :] * src
            acc = t if acc is None else acc + t
        yl = _layernorm_rows(acc + bcc_ref[...], gcln_ref[...], bcln_ref[...])
        y_ref[pl.ds(r, rc), 2 * C:3 * C] = (yl * jax.nn.sigmoid(yl)).astype(BF16)

    for r in range(0, tt, rc):
        rows = pl.ds(r, rc)
        v = _layernorm_rows(z_ref[rows, OFF_GM + C:OFF_GM + 2 * C].astype(F32),
                            gvln_ref[...], bvln_ref[...])
        if emit_v:
            v_ref[rows, :] = v
        vb_s[rows, :] = v.astype(BF16)
    ri = lax.broadcasted_iota(jnp.int32, (chunk, chunk), 0)
    ci = lax.broadcasted_iota(jnp.int32, (chunk, chunk), 1)
    for hh in range(N_GMLP_HEADS):
        lanes = slice(hh * GMLP_HEAD, (hh + 1) * GMLP_HEAD)
        wsm = jnp.where(ci <= ri, ws_ref[hh], 0.0).astype(BF16)
        for n in range(tt // chunk):
            rows = pl.ds(n * chunk, chunk)
            mixed = jnp.dot(wsm, vb_s[rows, lanes], preferred_element_type=F32) + bs_ref[:, lanes]
            u = z_ref[rows, OFF_GM + hh * GMLP_HEAD:OFF_GM + (hh + 1) * GMLP_HEAD].astype(F32)
            y_ref[rows, 3 * C + hh * GMLP_HEAD:3 * C + (hh + 1) * GMLP_HEAD] = (u * mixed).astype(BF16)

    for ext, out in ((pext, npool_ref), (scext, nsc_ref), (ccext, ncc_ref)):
        tail = ext[tt:tt + HALO, :]
        ext[0:HALO, :] = tail
        out[0] = tail


def _mixers(z, *, n_seq, t, tt, row_offset, start_pos, states, y_prev, lw, emit_v):
    m = z.shape[0]
    C = D_BRANCH
    assert t % tt == 0 and tt >= HALO and row_offset % tt == 0
    assert HALO >= CCONV_W - 1 and HALO % V7X_SUBLANES == 0 and tt % V7X_SUBLANES == 0
    chunk = min(t, GMLP_CHUNK)
    assert tt % chunk == 0
    nt = t // tt
    blk0 = row_offset // tt
    has_state = states is not None
    has_prev = y_prev is not None
    sh_rows = tt + HALO - V7X_SUBLANES

    def row_map(b, s):
        return (blk0 + b * nt + s, 0)

    def const2(b, s):
        return (0, 0)

    def const3(b, s):
        return (0, 0, 0)

    def seq3(b, s):
        return (b, 0, 0)

    in_specs = [pl.BlockSpec((tt, MIX_COLS), row_map)]
    args = [z]
    if has_state:
        for st in states:
            in_specs.append(pl.BlockSpec((1, HALO, C), seq3))
            args.append(st)
    if has_prev:
        in_specs.append(pl.BlockSpec(memory_space=pl.ANY))
        args.append(y_prev)
    ws = lw["w_spatial"][:, :chunk, :chunk]
    bs = jnp.repeat(lw["b_spatial"][:, :chunk].T, GMLP_HEAD, axis=1)
    small = [
        (lw["w_pool"].astype(BF16), const3), (lw["pool_scale"].reshape(1, C), const2),
        (lw["w_sconv"], const2), (lw["w_cconv"], const2), (lw["b_cconv"].reshape(1, C), const2),
        (lw["g_cln"].reshape(1, C), const2), (lw["b_cln"].reshape(1, C), const2),
        (lw["g_vln"].reshape(1, C), const2), (lw["b_vln"].reshape(1, C), const2),
        (ws, const3), (bs, const2),
    ]
    for arr, imap in small:
        in_specs.append(pl.BlockSpec(arr.shape, imap))
        args.append(arr)

    out_shape = [jax.ShapeDtypeStruct((m, D_MODEL), BF16)] + [
        jax.ShapeDtypeStruct((n_seq, HALO, C), F32)] * 3
    out_specs = [pl.BlockSpec((tt, D_MODEL), row_map)] + [pl.BlockSpec((1, HALO, C), seq3)] * 3
    if emit_v:
        out_shape.append(jax.ShapeDtypeStruct((n_seq * t, C), F32))
        out_specs.append(pl.BlockSpec((tt, C), lambda b, s: (b * nt + s, 0)))

    aliases = {}
    if has_prev:
        aliases = {1 + (3 if has_state else 0): 0}
    nbytes = (2 * (tt * MIX_COLS * 2 + tt * D_MODEL * 2 + tt * C * 4)
              + 3 * (tt + HALO) * C * 4 + (V7X_SUBLANES - 1) * sh_rows * C * 4 + 2 * tt * C * 2
              + 16 * HALO * C * 4
              + 16 * tt * C * 4)
    kern = functools.partial(_mixer_kernel, tt=tt, chunk=chunk, start_pos=start_pos,
                             has_state=has_state, has_prev=has_prev, emit_v=emit_v)
    return pl.pallas_call(
        kern,
        grid=(n_seq, nt),
        in_specs=in_specs,
        out_specs=out_specs,
        out_shape=out_shape,
        scratch_shapes=[pltpu.VMEM((HALO + tt, C), F32)] * 3
        + [pltpu.VMEM((V7X_SUBLANES - 1, sh_rows, C), F32)]
        + [pltpu.VMEM((tt, C), BF16)] * 2,
        input_output_aliases=aliases,
        compiler_params=pltpu.CompilerParams(
            dimension_semantics=("arbitrary", "arbitrary"), vmem_limit_bytes=_vmem_limit(nbytes)),
        name="mixers_state" if has_state else "mixers",
    )(*args)


def _pad_state(st):
    return jnp.pad(st, ((0, 0), (HALO - st.shape[1], 0), (0, 0)))


def kernel(x_prompt, x_sample, state_pool, state_sconv, state_cconv, g_mix, w_in, w_pool,
           pool_scale, w_sconv, w_cconv, b_cconv, g_cln, b_cln, g_vln, b_vln, w_spatial,
           b_spatial, w_branch, w_out, g_ffn, w_ff1, w_ff2, g_final):
    bp, tp, d = x_prompt.shape
    bs_, ts, _ = x_sample.shape
    assert d == D_MODEL and ts <= 64
    mp, ms = bp * tp, bs_ * ts
    m = mp + ms

    tm = _row_tile(m, 1100)
    tt_p = 256
    tt_s = ts
    tr = _row_tile(ms, 256)
    assert mp % tr == 0

    w_in_b, w_branch_b, w_out_b = w_in.astype(BF16), w_branch.astype(BF16), w_out.astype(BF16)
    w_ff1_b, w_ff2_b = w_ff1.astype(BF16), w_ff2.astype(BF16)

    x, xb, ss = _prep(x_prompt.reshape(mp, d), x_sample.reshape(ms, d), g_mix[0], tr=tr)

    new_pool_p, new_pool_s, new_sc_p, new_sc_s, new_cc_p, new_cc_s, vs = [], [], [], [], [], [], []
    for l in range(DEPTH):
        lw = dict(w_pool=w_pool[l], pool_scale=pool_scale[l], w_sconv=w_sconv[l],
                  w_cconv=w_cconv[l], b_cconv=b_cconv[l], g_cln=g_cln[l], b_cln=b_cln[l],
                  g_vln=g_vln[l], b_vln=b_vln[l], w_spatial=w_spatial[l], b_spatial=b_spatial[l])
        z = _matmul_norm(xb, w_in_b, l, ss, tm=tm, tn=1024, name="in_proj")

        y, npp, nsp, ncp = _mixers(z, n_seq=bp, t=tp, tt=tt_p, row_offset=0, start_pos=0,
                                   states=None, y_prev=None, lw=lw, emit_v=False)
        states = (_pad_state(state_pool[l]), _pad_state(state_sconv[l]), _pad_state(state_cconv[l]))
        y, nps, nss, ncs, v = _mixers(z, n_seq=bs_, t=ts, tt=tt_s, row_offset=mp,
                                      start_pos=PAST_LEN, states=states, y_prev=y, lw=lw,
                                      emit_v=True)
        new_pool_p.append(npp[:, HALO - POOL_STATE:])
        new_pool_s.append(nps[:, HALO - POOL_STATE:])
        new_sc_p.append(nsp[:, HALO - (SCONV_W - 1):])
        new_sc_s.append(nss[:, HALO - (SCONV_W - 1):])
        new_cc_p.append(ncp[:, HALO - (CCONV_W - 1):])
        new_cc_s.append(ncs[:, HALO - (CCONV_W - 1):])
        vs.append(v.reshape(bs_, ts, D_BRANCH))

        merged = _branch_merge(y, z, w_branch_b, l, tm=tm, tn=512)
        x, xb, ss = _matmul_res_emit(merged, w_out_b, l, x, g_ffn[l], tm=tm, tn=512,
                                     name="out_proj")
        hid = _matmul_norm(xb, w_ff1_b, l, ss, tm=tm, tn=1024, act="relu2", name="ff1")
        if l + 1 < DEPTH:
            x, xb, ss = _matmul_kacc(hid, w_ff2_b, l, x, g_mix[l + 1], tm=tm, tn=1024, tk=2048,
                                     name="ff2")
        else:
            x = _matmul_kacc(hid, w_ff2_b, l, x, None, tm=tm, tn=1024, tk=2048, name="ff2_last")

    y_prompt = _rmsnorm(x, g_final, out_dtype=F32, rows=mp, row_block_offset=0, tr=tr)
    y_sample = _rmsnorm(x, g_final, out_dtype=F32, rows=ms, row_block_offset=mp // tr, tr=tr)
    return (y_prompt.reshape(bp, tp, d), y_sample.reshape(bs_, ts, d),
            jnp.stack(new_pool_p), jnp.stack(new_pool_s),
            jnp.stack(new_sc_p), jnp.stack(new_sc_s),
            jnp.stack(new_cc_p), jnp.stack(new_cc_s),
            jnp.stack(vs))
```

```python
import functools

import jax
import jax.numpy as jnp
from jax import lax
from jax.experimental import pallas as pl
from jax.experimental.pallas import tpu as pltpu

F32 = jnp.float32
BF16 = jnp.bfloat16

D_MODEL = 4096
DEPTH = 2
PAST_LEN = 2048
N_BRANCH = 4
D_BRANCH = D_MODEL // N_BRANCH
POOL_WINDOWS = (2, 4, 8, 16)
POOL_GROUP = D_BRANCH // len(POOL_WINDOWS)
POOL_STATE = max(POOL_WINDOWS) - 1
SCONV_W = 3
CCONV_W = 31
GMLP_CHUNK = 128
N_GMLP_HEADS = 8
GMLP_HEAD = D_BRANCH // N_GMLP_HEADS
D_FF = 4 * D_MODEL
EPS = 1e-6
OFF_POOL = 0
OFF_SC = OFF_POOL + D_BRANCH
OFF_CC = OFF_SC + 3 * D_BRANCH
OFF_GM = OFF_CC + 2 * D_BRANCH
OFF_GATE = OFF_GM + 2 * D_BRANCH
IN_COLS = OFF_GATE + N_BRANCH * D_MODEL
MIX_COLS = OFF_GATE

V7X_VMEM_BYTES = 64 * 1024 * 1024
V7X_LANES = 128
V7X_SUBLANES = 8
V7X_SUBLANES_BF16 = 16
MiB = 1024 * 1024

HALO = 32
MIX_RC = 32
TM_MAX = 1024
TR_MAX = 256


def _vmem_limit(nbytes):
    return int(min(V7X_VMEM_BYTES - 4 * MiB, nbytes + 12 * MiB))


def _row_tile(m, target):
    best = None
    for t in range(V7X_SUBLANES_BF16, target + 1, V7X_SUBLANES_BF16):
        if m % t == 0:
            best = t
    assert best is not None, (m, target)
    return best


def _fold_lanes(v):
    acc = v[:, 0:V7X_LANES]
    for c in range(1, v.shape[1] // V7X_LANES):
        acc = acc + v[:, c * V7X_LANES:(c + 1) * V7X_LANES]
    return acc


def _row_scale(ss_ref):
    return lax.rsqrt(jnp.sum(ss_ref[...], axis=-1, keepdims=True) * (1.0 / D_MODEL) + EPS)


def _emit_stream(x, g_ref, xb_ref, ss_ref, first):
    xb_ref[...] = (x * g_ref[...]).astype(BF16)
    part = _fold_lanes(x * x)

    @pl.when(first)
    def _():
        ss_ref[...] = part

    @pl.when(jnp.logical_not(first))
    def _():
        ss_ref[...] += part


def _weight_spec(w, layer, blk, idx):
    if w.dtype == BF16:
        return pl.BlockSpec(blk, idx)
    return pl.BlockSpec((None,) + blk, lambda *g: (layer,) + idx(*g))


def _load_weight(w_ref, wb_ref):
    w = w_ref[...]
    if wb_ref is not None:
        w = w.astype(BF16)
        wb_ref[...] = w
    return w


def _prep_kernel(x_ref, g_ref, xb_ref, ss_ref):
    x = x_ref[...]
    xb_ref[...] = (x * g_ref[...]).astype(BF16)
    ss_ref[...] = _fold_lanes(x * x)


def _prep(x, g, *, tr):
    m, d = x.shape
    assert m % tr == 0
    nbytes = 2 * tr * d * (4 + 2) + 2 * tr * d * 4
    return pl.pallas_call(
        _prep_kernel,
        grid=(m // tr,),
        in_specs=[pl.BlockSpec((tr, d), lambda i: (i, 0)),
                  pl.BlockSpec((1, d), lambda i: (0, 0))],
        out_specs=[pl.BlockSpec((tr, d), lambda i: (i, 0)),
                   pl.BlockSpec((tr, V7X_LANES), lambda i: (i, 0))],
        out_shape=[jax.ShapeDtypeStruct((m, d), BF16), jax.ShapeDtypeStruct((m, V7X_LANES), F32)],
        compiler_params=pltpu.CompilerParams(
            dimension_semantics=("arbitrary",), vmem_limit_bytes=_vmem_limit(nbytes)),
        name="prep",
    )(x, g.reshape(1, d))


def _rmsnorm_kernel(x_ref, g_ref, o_ref):
    x = x_ref[...]
    ms = jnp.mean(x * x, axis=-1, keepdims=True)
    o_ref[...] = (x * lax.rsqrt(ms + EPS) * g_ref[...]).astype(o_ref.dtype)


def _rmsnorm(x, g, *, tr):
    m, d = x.shape
    assert m % tr == 0
    nbytes = 2 * tr * d * (4 + 4) + 2 * tr * d * 4
    return pl.pallas_call(
        _rmsnorm_kernel,
        grid=(m // tr,),
        in_specs=[pl.BlockSpec((tr, d), lambda i: (i, 0)),
                  pl.BlockSpec((1, d), lambda i: (0, 0))],
        out_specs=pl.BlockSpec((tr, d), lambda i: (i, 0)),
        out_shape=jax.ShapeDtypeStruct((m, d), F32),
        compiler_params=pltpu.CompilerParams(
            dimension_semantics=("arbitrary",), vmem_limit_bytes=_vmem_limit(nbytes)),
        name="rmsnorm",
    )(x, g.reshape(1, d))


def _mm_norm_kernel(a_ref, w_ref, ss_ref, o_ref, wb_ref=None, *, act):
    acc = jnp.dot(a_ref[...], _load_weight(w_ref, wb_ref), preferred_element_type=F32)
    acc = acc * _row_scale(ss_ref)
    if act == "relu2":
        acc = jnp.square(jnp.maximum(acc, 0.0))
    o_ref[...] = acc.astype(o_ref.dtype)


def _matmul_norm(a, w, layer, ss, *, tm, tn, act=None, name):
    m, k = a.shape
    k2, n = w.shape[-2:]
    cast = w.dtype != BF16
    assert k == k2 and m % tm == 0 and n % tn == 0 and (not cast or m == tm)
    wsz = 4 if cast else 2
    nbytes = (2 * (tm * k * 2 + k * tn * wsz + tm * tn * 2 + tm * V7X_LANES * 4) + tm * tn * 4
              + (3 * k * tn * 2 if cast else 0))
    out_specs = [pl.BlockSpec((tm, tn), lambda i, j: (i, j))]
    out_shape = [jax.ShapeDtypeStruct((m, n), BF16)]
    if cast:
        out_specs.append(pl.BlockSpec((k, tn), lambda i, j: (0, j)))
        out_shape.append(jax.ShapeDtypeStruct((k, n), BF16))
    out = pl.pallas_call(
        functools.partial(_mm_norm_kernel, act=act),
        grid=(m // tm, n // tn),
        in_specs=[pl.BlockSpec((tm, k), lambda i, j: (i, 0)),
                  _weight_spec(w, layer, (k, tn), lambda i, j: (0, j)),
                  pl.BlockSpec((tm, V7X_LANES), lambda i, j: (i, 0))],
        out_specs=out_specs,
        out_shape=out_shape,
        compiler_params=pltpu.CompilerParams(
            dimension_semantics=("arbitrary", "arbitrary"), vmem_limit_bytes=_vmem_limit(nbytes)),
        name=name,
    )(a, w, ss)
    return out if cast else out[0]


def _mm_res_emit_kernel(a_ref, w_ref, r_ref, g_ref, o_ref, xb_ref, ss_ref, wb_ref=None):
    x = r_ref[...] + jnp.dot(a_ref[...], _load_weight(w_ref, wb_ref), preferred_element_type=F32)
    o_ref[...] = x
    _emit_stream(x, g_ref, xb_ref, ss_ref, pl.program_id(1) == 0)


def _matmul_res_emit(a, w, layer, res, g, *, tm, tn, name):
    m, k = a.shape
    k2, n = w.shape[-2:]
    cast = w.dtype != BF16
    assert k == k2 and m % tm == 0 and n % tn == 0 and (not cast or m == tm)
    wsz = 4 if cast else 2
    nbytes = (2 * (tm * k * 2 + k * tn * wsz + tm * tn * (4 + 4 + 2) + tm * V7X_LANES * 4)
              + 2 * tm * tn * 4 + (3 * k * tn * 2 if cast else 0))
    out_specs = [pl.BlockSpec((tm, tn), lambda i, j: (i, j)),
                 pl.BlockSpec((tm, tn), lambda i, j: (i, j)),
                 pl.BlockSpec((tm, V7X_LANES), lambda i, j: (i, 0))]
    out_shape = [jax.ShapeDtypeStruct((m, n), F32), jax.ShapeDtypeStruct((m, n), BF16),
                 jax.ShapeDtypeStruct((m, V7X_LANES), F32)]
    if cast:
        out_specs.append(pl.BlockSpec((k, tn), lambda i, j: (0, j)))
        out_shape.append(jax.ShapeDtypeStruct((k, n), BF16))
    return pl.pallas_call(
        _mm_res_emit_kernel,
        grid=(m // tm, n // tn),
        in_specs=[pl.BlockSpec((tm, k), lambda i, j: (i, 0)),
                  _weight_spec(w, layer, (k, tn), lambda i, j: (0, j)),
                  pl.BlockSpec((tm, tn), lambda i, j: (i, j)),
                  pl.BlockSpec((1, tn), lambda i, j: (0, j))],
        out_specs=out_specs,
        out_shape=out_shape,
        compiler_params=pltpu.CompilerParams(
            dimension_semantics=("arbitrary", "arbitrary"), vmem_limit_bytes=_vmem_limit(nbytes)),
        name=name,
    )(a, w, res, g.reshape(1, n))


def _mm_kacc_kernel(a_ref, w_ref, r_ref, *rest, emit, cast):
    rest = list(rest)
    g_ref = rest.pop(0) if emit else None
    o_ref = rest.pop(0)
    xb_ref, ss_ref = (rest.pop(0), rest.pop(0)) if emit else (None, None)
    wb_ref = rest.pop(0) if cast else None
    kk = pl.program_id(2)

    @pl.when(kk == 0)
    def _():
        o_ref[...] = r_ref[...]

    o_ref[...] += jnp.dot(a_ref[...], _load_weight(w_ref, wb_ref), preferred_element_type=F32)

    if emit:
        @pl.when(kk == pl.num_programs(2) - 1)
        def _():
            _emit_stream(o_ref[...], g_ref, xb_ref, ss_ref, pl.program_id(1) == 0)


def _matmul_kacc(a, w, layer, res, g, *, tm, tn, tk, name):
    m, k = a.shape
    k2, n = w.shape[-2:]
    cast = w.dtype != BF16
    emit = g is not None
    assert k == k2 and m % tm == 0 and n % tn == 0 and k % tk == 0 and (not cast or m == tm)
    wsz = 4 if cast else 2
    nbytes = (2 * (tm * tk * 2 + tk * tn * wsz + 2 * tm * tn * 4) + 2 * tm * tn * 4
              + (3 * tk * tn * 2 if cast else 0))
    in_specs = [pl.BlockSpec((tm, tk), lambda i, j, kk: (i, kk)),
                _weight_spec(w, layer, (tk, tn), lambda i, j, kk: (kk, j)),
                pl.BlockSpec((tm, tn), lambda i, j, kk: (i, j))]
    args = [a, w, res]
    out_specs = [pl.BlockSpec((tm, tn), lambda i, j, kk: (i, j))]
    out_shape = [jax.ShapeDtypeStruct((m, n), F32)]
    if emit:
        in_specs.append(pl.BlockSpec((1, tn), lambda i, j, kk: (0, j)))
        args.append(g.reshape(1, n))
        out_specs += [pl.BlockSpec((tm, tn), lambda i, j, kk: (i, j)),
                      pl.BlockSpec((tm, V7X_LANES), lambda i, j, kk: (i, 0))]
        out_shape += [jax.ShapeDtypeStruct((m, n), BF16), jax.ShapeDtypeStruct((m, V7X_LANES), F32)]
        nbytes += 2 * (tm * tn * 2 + tm * V7X_LANES * 4)
    if cast:
        out_specs.append(pl.BlockSpec((tk, tn), lambda i, j, kk: (kk, j)))
        out_shape.append(jax.ShapeDtypeStruct((k, n), BF16))
    return pl.pallas_call(
        functools.partial(_mm_kacc_kernel, emit=emit, cast=cast),
        grid=(m // tm, n // tn, k // tk),
        in_specs=in_specs,
        out_specs=out_specs,
        out_shape=out_shape,
        compiler_params=pltpu.CompilerParams(
            dimension_semantics=("arbitrary", "arbitrary", "arbitrary"),
            vmem_limit_bytes=_vmem_limit(nbytes)),
        name=name,
    )(*args)


def _branch_kernel(y_ref, g0_ref, g1_ref, g2_ref, g3_ref, wb_in_ref, o_ref, wb_out_ref=None):
    acc = None
    for i, g_ref in enumerate((g0_ref, g1_ref, g2_ref, g3_ref)):
        w = wb_in_ref[i]
        if wb_out_ref is not None:
            w = w.astype(BF16)
            wb_out_ref[i] = w
        d = jnp.dot(y_ref[:, i * D_BRANCH:(i + 1) * D_BRANCH], w, preferred_element_type=F32)
        t = jax.nn.sigmoid(g_ref[...].astype(F32)) * d
        acc = t if acc is None else acc + t
    o_ref[...] = acc.astype(o_ref.dtype)


def _branch_merge(y, z, wb, layer, *, tm, tn):
    m = y.shape[0]
    cast = wb.dtype != BF16
    assert m % tm == 0 and D_MODEL % tn == 0 and OFF_GATE % tn == 0 and (not cast or m == tm)
    wsz = 4 if cast else 2
    nbytes = (2 * (tm * D_MODEL * 2 + N_BRANCH * tm * tn * 2 + N_BRANCH * D_BRANCH * tn * wsz
                   + tm * tn * 2) + 3 * tm * tn * 4 + (3 * N_BRANCH * D_BRANCH * tn * 2 if cast else 0))

    def gate_spec(i):
        base = (OFF_GATE + i * D_MODEL) // tn
        return pl.BlockSpec((tm, tn), lambda r, j: (r, base + j))

    wblk = (N_BRANCH, D_BRANCH, tn)
    out_specs = [pl.BlockSpec((tm, tn), lambda r, j: (r, j))]
    out_shape = [jax.ShapeDtypeStruct((m, D_MODEL), BF16)]
    if cast:
        out_specs.append(pl.BlockSpec(wblk, lambda r, j: (0, 0, j)))
        out_shape.append(jax.ShapeDtypeStruct((N_BRANCH, D_BRANCH, D_MODEL), BF16))
    out = pl.pallas_call(
        _branch_kernel,
        grid=(m // tm, D_MODEL // tn),
        in_specs=[pl.BlockSpec((tm, D_MODEL), lambda r, j: (r, 0))]
        + [gate_spec(i) for i in range(N_BRANCH)]
        + [_weight_spec(wb, layer, wblk, lambda r, j: (0, 0, j))],
        out_specs=out_specs,
        out_shape=out_shape,
        compiler_params=pltpu.CompilerParams(
            dimension_semantics=("arbitrary", "arbitrary"), vmem_limit_bytes=_vmem_limit(nbytes)),
        name="branch_merge",
    )(y, z, z, z, z, wb)
    return out if cast else out[0]


def _layernorm_rows(x, g, b):
    mu = jnp.mean(x, axis=-1, keepdims=True)
    xc = x - mu
    var = jnp.mean(xc * xc, axis=-1, keepdims=True)
    return xc * lax.rsqrt(var + EPS) * g + b


def _mixer_kernel(*refs, tt, chunk, start_pos, has_state, emit_v):
    refs = list(refs)
    z_ref = refs.pop(0)
    if has_state:
        stp_ref, sts_ref, stc_ref = refs.pop(0), refs.pop(0), refs.pop(0)
    (wpool_ref, pscale_ref, wsc_ref, wcc_ref, bcc_ref, gcln_ref, bcln_ref, gvln_ref, bvln_ref,
     ws_ref, bs_ref) = refs[:11]
    refs = refs[11:]
    y_ref, npool_ref, nsc_ref, ncc_ref = refs[:4]
    refs = refs[4:]
    if emit_v:
        v_ref = refs.pop(0)
    pext, scext, ccext, ccsh, pooled_s, vb_s = refs

    s = pl.program_id(1)
    C = D_BRANCH
    rc = min(MIX_RC, tt)

    @pl.when(s == 0)
    def _():
        if has_state:
            pext[0:HALO, :] = stp_ref[0]
            scext[0:HALO, :] = sts_ref[0]
            ccext[0:HALO, :] = stc_ref[0]
        else:
            zero = jnp.zeros((HALO, C), F32)
            pext[0:HALO, :] = zero
            scext[0:HALO, :] = zero
            ccext[0:HALO, :] = zero

    for r in range(0, tt, rc):
        rows = pl.ds(r, rc)
        dst = pl.ds(HALO + r, rc)
        pext[dst, :] = z_ref[rows, OFF_POOL:OFF_POOL + C].astype(F32)
        h = z_ref[rows, OFF_SC:OFF_SC + C].astype(F32)
        cg = z_ref[rows, OFF_SC + 2 * C:OFF_SC + 3 * C].astype(F32)
        scext[dst, :] = cg * h
        a = z_ref[rows, OFF_CC:OFF_CC + C].astype(F32)
        g = z_ref[rows, OFF_CC + C:OFF_CC + 2 * C].astype(F32)
        ccext[dst, :] = a * jax.nn.sigmoid(g)

    pos0 = start_pos + s * tt
    for r in range(0, tt, rc):
        pos = pos0 + r + lax.broadcasted_iota(jnp.int32, (rc, POOL_GROUP), 0)
        for gi, w in enumerate(POOL_WINDOWS):
            lanes = slice(gi * POOL_GROUP, (gi + 1) * POOL_GROUP)
            tok = pext[pl.ds(HALO + r, rc), lanes]
            tot = tok
            for k in range(1, w):
                tot = tot + pext[pl.ds(HALO + r - k, rc), lanes]
            cnt = jnp.minimum(pos + 1, w).astype(F32)
            pooled_s[pl.ds(r, rc), lanes] = (tot / cnt - tok).astype(BF16)
    for gi in range(len(POOL_WINDOWS)):
        lanes = slice(gi * POOL_GROUP, (gi + 1) * POOL_GROUP)
        ya = jnp.dot(pooled_s[:, lanes], wpool_ref[gi].astype(BF16), preferred_element_type=F32)
        y_ref[:, lanes] = (ya * pscale_ref[:, lanes]).astype(BF16)

    for r in range(0, tt, rc):
        rows = pl.ds(r, rc)
        acc = None
        for k in range(SCONV_W):
            t = wsc_ref[k:k + 1, :] * scext[pl.ds(HALO + r - (SCONV_W - 1) + k, rc), :]
            acc = t if acc is None else acc + t
        bg = z_ref[rows, OFF_SC + C:OFF_SC + 2 * C].astype(F32)
        y_ref[rows, C:2 * C] = (bg * acc).astype(BF16)

    sh_rows = ccsh.shape[1]
    for b in range(1, V7X_SUBLANES):
        for r in range(0, sh_rows, rc):
            n = min(rc, sh_rows - r)
            ccsh[b - 1, pl.ds(r, n), :] = ccext[pl.ds(r + b, n), :]
    for r in range(0, tt, rc):
        acc = None
        for k in range(CCONV_W):
            o = k + HALO - (CCONV_W - 1)
            o8, ob = (o // V7X_SUBLANES) * V7X_SUBLANES, o % V7X_SUBLANES
            src = ccext[pl.ds(r + o8, rc), :] if ob == 0 else ccsh[ob - 1, pl.ds(r + o8, rc), :]
            t = wcc_ref[k:k + 1, :] * src
            acc = t if acc is None else acc + t
        yl = _layernorm_rows(acc + bcc_ref[...], gcln_ref[...], bcln_ref[...])
        y_ref[pl.ds(r, rc), 2 * C:3 * C] = (yl * jax.nn.sigmoid(yl)).astype(BF16)

    for r in range(0, tt, rc):
        rows = pl.ds(r, rc)
        v = _layernorm_rows(z_ref[rows, OFF_GM + C:OFF_GM + 2 * C].astype(F32),
                            gvln_ref[...], bvln_ref[...])
        if emit_v:
            v_ref[rows, :] = v
        vb_s[rows, :] = v.astype(BF16)
    ri = lax.broadcasted_iota(jnp.int32, (chunk, chunk), 0)
    ci = lax.broadcasted_iota(jnp.int32, (chunk, chunk), 1)
    for hh in range(N_GMLP_HEADS):
        lanes = slice(hh * GMLP_HEAD, (hh + 1) * GMLP_HEAD)
        wsm = jnp.where(ci <= ri, ws_ref[hh], 0.0).astype(BF16)
        for n in range(tt // chunk):
            rows = pl.ds(n * chunk, chunk)
            mixed = jnp.dot(wsm, vb_s[rows, lanes], preferred_element_type=F32) + bs_ref[:, lanes]
            u = z_ref[rows, OFF_GM + hh * GMLP_HEAD:OFF_GM + (hh + 1) * GMLP_HEAD].astype(F32)
            y_ref[rows, 3 * C + hh * GMLP_HEAD:3 * C + (hh + 1) * GMLP_HEAD] = (u * mixed).astype(BF16)

    for ext, out in ((pext, npool_ref), (scext, nsc_ref), (ccext, ncc_ref)):
        tail = ext[tt:tt + HALO, :]
        ext[0:HALO, :] = tail
        out[0] = tail


def _mixers(z, *, n_seq, t, tt, start_pos, states, lw, emit_v):
    m = z.shape[0]
    C = D_BRANCH
    assert m == n_seq * t and t % tt == 0 and tt >= HALO
    assert HALO >= CCONV_W - 1 and HALO % V7X_SUBLANES == 0 and tt % V7X_SUBLANES == 0
    chunk = min(t, GMLP_CHUNK)
    assert tt % chunk == 0
    nt = t // tt
    has_state = states is not None
    sh_rows = tt + HALO - V7X_SUBLANES

    def row_map(b, s):
        return (b * nt + s, 0)

    def const2(b, s):
        return (0, 0)

    def const3(b, s):
        return (0, 0, 0)

    def seq3(b, s):
        return (b, 0, 0)

    in_specs = [pl.BlockSpec((tt, MIX_COLS), row_map)]
    args = [z]
    if has_state:
        for st in states:
            in_specs.append(pl.BlockSpec((1, HALO, C), seq3))
            args.append(st)
    ws = lw["w_spatial"][:, :chunk, :chunk]
    bs = jnp.repeat(lw["b_spatial"][:, :chunk].T, GMLP_HEAD, axis=1)
    small = [
        (lw["w_pool"], const3), (lw["pool_scale"].reshape(1, C), const2),
        (lw["w_sconv"], const2), (lw["w_cconv"], const2), (lw["b_cconv"].reshape(1, C), const2),
        (lw["g_cln"].reshape(1, C), const2), (lw["b_cln"].reshape(1, C), const2),
        (lw["g_vln"].reshape(1, C), const2), (lw["b_vln"].reshape(1, C), const2),
        (ws, const3), (bs, const2),
    ]
    for arr, imap in small:
        in_specs.append(pl.BlockSpec(arr.shape, imap))
        args.append(arr)

    out_shape = [jax.ShapeDtypeStruct((m, D_MODEL), BF16)] + [
        jax.ShapeDtypeStruct((n_seq, HALO, C), F32)] * 3
    out_specs = [pl.BlockSpec((tt, D_MODEL), row_map)] + [pl.BlockSpec((1, HALO, C), seq3)] * 3
    if emit_v:
        out_shape.append(jax.ShapeDtypeStruct((m, C), F32))
        out_specs.append(pl.BlockSpec((tt, C), row_map))

    nbytes = (2 * (tt * MIX_COLS * 2 + tt * D_MODEL * 2 + tt * C * 4)
              + 3 * (tt + HALO) * C * 4 + (V7X_SUBLANES - 1) * sh_rows * C * 4 + 2 * tt * C * 2
              + 16 * HALO * C * 4
              + 16 * tt * C * 4)
    kern = functools.partial(_mixer_kernel, tt=tt, chunk=chunk, start_pos=start_pos,
                             has_state=has_state, emit_v=emit_v)
    return pl.pallas_call(
        kern,
        grid=(n_seq, nt),
        in_specs=in_specs,
        out_specs=out_specs,
        out_shape=out_shape,
        scratch_shapes=[pltpu.VMEM((HALO + tt, C), F32)] * 3
        + [pltpu.VMEM((V7X_SUBLANES - 1, sh_rows, C), F32)]
        + [pltpu.VMEM((tt, C), BF16)] * 2,
        compiler_params=pltpu.CompilerParams(
            dimension_semantics=("arbitrary", "arbitrary"), vmem_limit_bytes=_vmem_limit(nbytes)),
        name="mixers_state" if has_state else "mixers",
    )(*args)


def _pad_state(st):
    return jnp.pad(st, ((0, 0), (HALO - st.shape[1], 0), (0, 0)))


def _layer(x, xb, ss, l, w, lw, g_next, *, n_seq, t, tt, start_pos, states, emit_v):
    m = x.shape[0]
    tm = _row_tile(m, TM_MAX)
    cast = w["in"].dtype != BF16
    wb = {}

    def split(out, n_main):
        out = list(out) if isinstance(out, (list, tuple)) else [out]
        return (out[:n_main], out[n_main] if cast else None)

    (z,), wb["in"] = split(_matmul_norm(xb, w["in"], l, ss, tm=tm, tn=1024 if not cast else 512,
                                        name="in_proj"), 1)
    mix = _mixers(z, n_seq=n_seq, t=t, tt=tt, start_pos=start_pos, states=states, lw=lw,
                  emit_v=emit_v)
    y, new_states, v = mix[0], mix[1:4], (mix[4] if emit_v else None)
    (merged,), wb["branch"] = split(_branch_merge(y, z, w["branch"], l, tm=tm, tn=512), 1)
    (x, xb, ss), wb["out"] = split(_matmul_res_emit(merged, w["out"], l, x, lw["g_ffn"], tm=tm,
                                                     tn=512, name="out_proj"), 3)
    (hid,), wb["ff1"] = split(_matmul_norm(xb, w["ff1"], l, ss, tm=tm, tn=1024 if not cast else 512,
                                           act="relu2", name="ff1"), 1)
    n_main = 1 if g_next is None else 3
    main, wb["ff2"] = split(_matmul_kacc(hid, w["ff2"], l, x, g_next, tm=tm, tn=1024, tk=2048,
                                         name="ff2" if g_next is not None else "ff2_last"), n_main)
    x, xb, ss = (main[0], None, None) if g_next is None else main
    return x, xb, ss, new_states, v, wb


def kernel(x_prompt, x_sample, state_pool, state_sconv, state_cconv, g_mix, w_in, w_pool,
           pool_scale, w_sconv, w_cconv, b_cconv, g_cln, b_cln, g_vln, b_vln, w_spatial,
           b_spatial, w_branch, w_out, g_ffn, w_ff1, w_ff2, g_final):
    bp, tp, d = x_prompt.shape
    bs_, ts, _ = x_sample.shape
    assert d == D_MODEL and ts <= 64
    mp, ms = bp * tp, bs_ * ts
    tt_p = 256
    tt_s = ts

    xp, xs = x_prompt.reshape(mp, d), x_sample.reshape(ms, d)
    xbp, ssp = _prep(xp, g_mix[0], tr=_row_tile(mp, TR_MAX))
    xbs, sss = _prep(xs, g_mix[0], tr=_row_tile(ms, TR_MAX))

    w_f32 = {"in": w_in, "branch": w_branch, "out": w_out, "ff1": w_ff1, "ff2": w_ff2}
    outs_p, outs_s, vs = [], [], []
    for l in range(DEPTH):
        lw = dict(w_pool=w_pool[l], pool_scale=pool_scale[l], w_sconv=w_sconv[l],
                  w_cconv=w_cconv[l], b_cconv=b_cconv[l], g_cln=g_cln[l], b_cln=b_cln[l],
                  g_vln=g_vln[l], b_vln=b_vln[l], w_spatial=w_spatial[l], b_spatial=b_spatial[l],
                  g_ffn=g_ffn[l])
        g_next = g_mix[l + 1] if l + 1 < DEPTH else None
        states = (_pad_state(state_pool[l]), _pad_state(state_sconv[l]), _pad_state(state_cconv[l]))
        xs, xbs, sss, st_s, v, w_b16 = _layer(xs, xbs, sss, l, w_f32, lw, g_next, n_seq=bs_, t=ts,
                                              tt=tt_s, start_pos=PAST_LEN, states=states,
                                              emit_v=True)
        xp, xbp, ssp, st_p, _, _ = _layer(xp, xbp, ssp, l, w_b16, lw, g_next, n_seq=bp, t=tp,
                                          tt=tt_p, start_pos=0, states=None, emit_v=False)
        outs_p.append(st_p)
        outs_s.append(st_s)
        vs.append(v.reshape(bs_, ts, D_BRANCH))

    y_prompt = _rmsnorm(xp, g_final, tr=_row_tile(mp, TR_MAX))
    y_sample = _rmsnorm(xs, g_final, tr=_row_tile(ms, TR_MAX))

    def stack(outs, idx, keep):
        return jnp.stack([o[idx][:, HALO - keep:] for o in outs])

    return (y_prompt.reshape(bp, tp, d), y_sample.reshape(bs_, ts, d),
            stack(outs_p, 0, POOL_STATE), stack(outs_s, 0, POOL_STATE),
            stack(outs_p, 1, SCONV_W - 1), stack(outs_s, 1, SCONV_W - 1),
            stack(outs_p, 2, CCONV_W - 1), stack(outs_s, 2, CCONV_W - 1),
            jnp.stack(vs))
```

```python
import functools

import jax
import jax.numpy as jnp
from jax import lax
from jax.experimental import pallas as pl
from jax.experimental.pallas import tpu as pltpu

F32 = jnp.float32
BF16 = jnp.bfloat16

D_MODEL = 4096
DEPTH = 2
PAST_LEN = 2048
N_BRANCH = 4
D_BRANCH = D_MODEL // N_BRANCH
POOL_WINDOWS = (2, 4, 8, 16)
POOL_GROUP = D_BRANCH // len(POOL_WINDOWS)
POOL_STATE = max(POOL_WINDOWS) - 1
SCONV_W = 3
CCONV_W = 31
GMLP_CHUNK = 128
N_GMLP_HEADS = 8
GMLP_HEAD = D_BRANCH // N_GMLP_HEADS
D_FF = 4 * D_MODEL
EPS = 1e-6
OFF_POOL = 0
OFF_SC = OFF_POOL + D_BRANCH
OFF_CC = OFF_SC + 3 * D_BRANCH
OFF_GM = OFF_CC + 2 * D_BRANCH
OFF_GATE = OFF_GM + 2 * D_BRANCH
IN_COLS = OFF_GATE + N_BRANCH * D_MODEL
MIX_COLS = OFF_GATE

V7X_VMEM_BYTES = 64 * 1024 * 1024
V7X_LANES = 128
V7X_SUBLANES = 8
V7X_SUBLANES_BF16 = 16
MiB = 1024 * 1024

HALO = 32
MIX_RC = 32
TM_MAX = 1024
TR_MAX = 256


def _vmem_limit(nbytes):
    return int(min(V7X_VMEM_BYTES - 2 * MiB, nbytes + 12 * MiB))


def _row_tile(m, target):
    best = None
    for t in range(V7X_SUBLANES_BF16, target + 1, V7X_SUBLANES_BF16):
        if m % t == 0:
            best = t
    assert best is not None, (m, target)
    return best


def _fold_lanes(v):
    acc = v[:, 0:V7X_LANES]
    for c in range(1, v.shape[1] // V7X_LANES):
        acc = acc + v[:, c * V7X_LANES:(c + 1) * V7X_LANES]
    return acc


def _row_scale(ss_ref):
    return lax.rsqrt(jnp.sum(ss_ref[...], axis=-1, keepdims=True) * (1.0 / D_MODEL) + EPS)


def _emit_stream(x, g_ref, xb_ref, ss_ref, first):
    xb_ref[...] = (x * g_ref[...]).astype(BF16)
    part = _fold_lanes(x * x)

    @pl.when(first)
    def _():
        ss_ref[...] = part

    @pl.when(jnp.logical_not(first))
    def _():
        ss_ref[...] += part


def _weight_spec(w, layer, blk, idx):
    if w.dtype == BF16:
        return pl.BlockSpec(blk, idx)
    return pl.BlockSpec((None,) + blk, lambda *g: (layer,) + idx(*g))


def _load_weight(w_ref, wb_ref):
    w = w_ref[...]
    if wb_ref is not None:
        w = w.astype(BF16)
        wb_ref[...] = w
    return w


def _prep_kernel(x_ref, g_ref, xb_ref, ss_ref):
    x = x_ref[...]
    xb_ref[...] = (x * g_ref[...]).astype(BF16)
    ss_ref[...] = _fold_lanes(x * x)


def _prep(x, g, *, tr):
    m, d = x.shape
    assert m % tr == 0
    nbytes = 2 * tr * d * (4 + 2) + 2 * tr * d * 4
    return pl.pallas_call(
        _prep_kernel,
        grid=(m // tr,),
        in_specs=[pl.BlockSpec((tr, d), lambda i: (i, 0)),
                  pl.BlockSpec((1, d), lambda i: (0, 0))],
        out_specs=[pl.BlockSpec((tr, d), lambda i: (i, 0)),
                   pl.BlockSpec((tr, V7X_LANES), lambda i: (i, 0))],
        out_shape=[jax.ShapeDtypeStruct((m, d), BF16), jax.ShapeDtypeStruct((m, V7X_LANES), F32)],
        compiler_params=pltpu.CompilerParams(
            dimension_semantics=("arbitrary",), vmem_limit_bytes=_vmem_limit(nbytes)),
        name="prep",
    )(x, g.reshape(1, d))


def _rmsnorm_kernel(x_ref, g_ref, o_ref):
    x = x_ref[...]
    ms = jnp.mean(x * x, axis=-1, keepdims=True)
    o_ref[...] = (x * lax.rsqrt(ms + EPS) * g_ref[...]).astype(o_ref.dtype)


def _rmsnorm(x, g, *, tr):
    m, d = x.shape
    assert m % tr == 0
    nbytes = 2 * tr * d * (4 + 4) + 2 * tr * d * 4
    return pl.pallas_call(
        _rmsnorm_kernel,
        grid=(m // tr,),
        in_specs=[pl.BlockSpec((tr, d), lambda i: (i, 0)),
                  pl.BlockSpec((1, d), lambda i: (0, 0))],
        out_specs=pl.BlockSpec((tr, d), lambda i: (i, 0)),
        out_shape=jax.ShapeDtypeStruct((m, d), F32),
        compiler_params=pltpu.CompilerParams(
            dimension_semantics=("arbitrary",), vmem_limit_bytes=_vmem_limit(nbytes)),
        name="rmsnorm",
    )(x, g.reshape(1, d))


def _mm_norm_kernel(a_ref, w_ref, ss_ref, o_ref, wb_ref=None, *, act):
    acc = jnp.dot(a_ref[...], _load_weight(w_ref, wb_ref), preferred_element_type=F32)
    acc = acc * _row_scale(ss_ref)
    if act == "relu2":
        acc = jnp.square(jnp.maximum(acc, 0.0))
    o_ref[...] = acc.astype(o_ref.dtype)


def _matmul_norm(a, w, layer, ss, *, tm, tn, act=None, name):
    m, k = a.shape
    k2, n = w.shape[-2:]
    cast = w.dtype != BF16
    assert k == k2 and m % tm == 0 and n % tn == 0 and (not cast or m == tm)
    wsz = 4 if cast else 2
    nbytes = (2 * (tm * k * 2 + k * tn * wsz + tm * tn * 2 + tm * V7X_LANES * 4) + tm * tn * 4
              + (3 * k * tn * 2 if cast else 0))
    out_specs = [pl.BlockSpec((tm, tn), lambda i, j: (i, j))]
    out_shape = [jax.ShapeDtypeStruct((m, n), BF16)]
    if cast:
        out_specs.append(pl.BlockSpec((k, tn), lambda i, j: (0, j)))
        out_shape.append(jax.ShapeDtypeStruct((k, n), BF16))
    out = pl.pallas_call(
        functools.partial(_mm_norm_kernel, act=act),
        grid=(m // tm, n // tn),
        in_specs=[pl.BlockSpec((tm, k), lambda i, j: (i, 0)),
                  _weight_spec(w, layer, (k, tn), lambda i, j: (0, j)),
                  pl.BlockSpec((tm, V7X_LANES), lambda i, j: (i, 0))],
        out_specs=out_specs,
        out_shape=out_shape,
        compiler_params=pltpu.CompilerParams(
            dimension_semantics=("arbitrary", "arbitrary"), vmem_limit_bytes=_vmem_limit(nbytes)),
        name=name,
    )(a, w, ss)
    return out if cast else out[0]


def _mm_res_emit_kernel(a_ref, w_ref, r_ref, g_ref, o_ref, xb_ref, ss_ref, wb_ref=None):
    x = r_ref[...] + jnp.dot(a_ref[...], _load_weight(w_ref, wb_ref), preferred_element_type=F32)
    o_ref[...] = x
    _emit_stream(x, g_ref, xb_ref, ss_ref, pl.program_id(1) == 0)


def _matmul_res_emit(a, w, layer, res, g, *, tm, tn, name):
    m, k = a.shape
    k2, n = w.shape[-2:]
    cast = w.dtype != BF16
    assert k == k2 and m % tm == 0 and n % tn == 0 and (not cast or m == tm)
    wsz = 4 if cast else 2
    nbytes = (2 * (tm * k * 2 + k * tn * wsz + tm * tn * (4 + 4 + 2) + tm * V7X_LANES * 4)
              + 2 * tm * tn * 4 + (3 * k * tn * 2 if cast else 0))
    out_specs = [pl.BlockSpec((tm, tn), lambda i, j: (i, j)),
                 pl.BlockSpec((tm, tn), lambda i, j: (i, j)),
                 pl.BlockSpec((tm, V7X_LANES), lambda i, j: (i, 0))]
    out_shape = [jax.ShapeDtypeStruct((m, n), F32), jax.ShapeDtypeStruct((m, n), BF16),
                 jax.ShapeDtypeStruct((m, V7X_LANES), F32)]
    if cast:
        out_specs.append(pl.BlockSpec((k, tn), lambda i, j: (0, j)))
        out_shape.append(jax.ShapeDtypeStruct((k, n), BF16))
    return pl.pallas_call(
        _mm_res_emit_kernel,
        grid=(m // tm, n // tn),
        in_specs=[pl.BlockSpec((tm, k), lambda i, j: (i, 0)),
                  _weight_spec(w, layer, (k, tn), lambda i, j: (0, j)),
                  pl.BlockSpec((tm, tn), lambda i, j: (i, j)),
                  pl.BlockSpec((1, tn), lambda i, j: (0, j))],
        out_specs=out_specs,
        out_shape=out_shape,
        compiler_params=pltpu.CompilerParams(
            dimension_semantics=("arbitrary", "arbitrary"), vmem_limit_bytes=_vmem_limit(nbytes)),
        name=name,
    )(a, w, res, g.reshape(1, n))


def _mm_kacc_kernel(a_ref, w_ref, r_ref, *rest, emit, cast):
    rest = list(rest)
    g_ref = rest.pop(0) if emit else None
    o_ref = rest.pop(0)
    xb_ref, ss_ref = (rest.pop(0), rest.pop(0)) if emit else (None, None)
    wb_ref = rest.pop(0) if cast else None
    kk = pl.program_id(2)

    @pl.when(kk == 0)
    def _():
        o_ref[...] = r_ref[...]

    o_ref[...] += jnp.dot(a_ref[...], _load_weight(w_ref, wb_ref), preferred_element_type=F32)

    if emit:
        @pl.when(kk == pl.num_programs(2) - 1)
        def _():
            _emit_stream(o_ref[...], g_ref, xb_ref, ss_ref, pl.program_id(1) == 0)


def _matmul_kacc(a, w, layer, res, g, *, tm, tn, tk, name):
    m, k = a.shape
    k2, n = w.shape[-2:]
    cast = w.dtype != BF16
    emit = g is not None
    assert k == k2 and m % tm == 0 and n % tn == 0 and k % tk == 0 and (not cast or m == tm)
    wsz = 4 if cast else 2
    nbytes = (2 * (tm * tk * 2 + tk * tn * wsz + 2 * tm * tn * 4) + 2 * tm * tn * 4
              + (3 * tk * tn * 2 if cast else 0))
    in_specs = [pl.BlockSpec((tm, tk), lambda i, j, kk: (i, kk)),
                _weight_spec(w, layer, (tk, tn), lambda i, j, kk: (kk, j)),
                pl.BlockSpec((tm, tn), lambda i, j, kk: (i, j))]
    args = [a, w, res]
    out_specs = [pl.BlockSpec((tm, tn), lambda i, j, kk: (i, j))]
    out_shape = [jax.ShapeDtypeStruct((m, n), F32)]
    if emit:
        in_specs.append(pl.BlockSpec((1, tn), lambda i, j, kk: (0, j)))
        args.append(g.reshape(1, n))
        out_specs += [pl.BlockSpec((tm, tn), lambda i, j, kk: (i, j)),
                      pl.BlockSpec((tm, V7X_LANES), lambda i, j, kk: (i, 0))]
        out_shape += [jax.ShapeDtypeStruct((m, n), BF16), jax.ShapeDtypeStruct((m, V7X_LANES), F32)]
        nbytes += 2 * (tm * tn * 2 + tm * V7X_LANES * 4)
    if cast:
        out_specs.append(pl.BlockSpec((tk, tn), lambda i, j, kk: (kk, j)))
        out_shape.append(jax.ShapeDtypeStruct((k, n), BF16))
    return pl.pallas_call(
        functools.partial(_mm_kacc_kernel, emit=emit, cast=cast),
        grid=(m // tm, n // tn, k // tk),
        in_specs=in_specs,
        out_specs=out_specs,
        out_shape=out_shape,
        compiler_params=pltpu.CompilerParams(
            dimension_semantics=("arbitrary", "arbitrary", "arbitrary"),
            vmem_limit_bytes=_vmem_limit(nbytes)),
        name=name,
    )(*args)


def _branch_kernel(y_ref, g0_ref, g1_ref, g2_ref, g3_ref, wb_in_ref, o_ref, wb_out_ref=None):
    acc = None
    for i, g_ref in enumerate((g0_ref, g1_ref, g2_ref, g3_ref)):
        w = wb_in_ref[i]
        if wb_out_ref is not None:
            w = w.astype(BF16)
            wb_out_ref[i] = w
        d = jnp.dot(y_ref[:, i * D_BRANCH:(i + 1) * D_BRANCH], w, preferred_element_type=F32)
        t = jax.nn.sigmoid(g_ref[...].astype(F32)) * d
        acc = t if acc is None else acc + t
    o_ref[...] = acc.astype(o_ref.dtype)


def _branch_merge(y, z, wb, layer, *, tm, tn):
    m = y.shape[0]
    cast = wb.dtype != BF16
    assert m % tm == 0 and D_MODEL % tn == 0 and OFF_GATE % tn == 0 and (not cast or m == tm)
    wsz = 4 if cast else 2
    nbytes = (2 * (tm * D_MODEL * 2 + N_BRANCH * tm * tn * 2 + N_BRANCH * D_BRANCH * tn * wsz
                   + tm * tn * 2) + 3 * tm * tn * 4 + (3 * N_BRANCH * D_BRANCH * tn * 2 if cast else 0))

    def gate_spec(i):
        base = (OFF_GATE + i * D_MODEL) // tn
        return pl.BlockSpec((tm, tn), lambda r, j: (r, base + j))

    wblk = (N_BRANCH, D_BRANCH, tn)
    out_specs = [pl.BlockSpec((tm, tn), lambda r, j: (r, j))]
    out_shape = [jax.ShapeDtypeStruct((m, D_MODEL), BF16)]
    if cast:
        out_specs.append(pl.BlockSpec(wblk, lambda r, j: (0, 0, j)))
        out_shape.append(jax.ShapeDtypeStruct((N_BRANCH, D_BRANCH, D_MODEL), BF16))
    out = pl.pallas_call(
        _branch_kernel,
        grid=(m // tm, D_MODEL // tn),
        in_specs=[pl.BlockSpec((tm, D_MODEL), lambda r, j: (r, 0))]
        + [gate_spec(i) for i in range(N_BRANCH)]
        + [_weight_spec(wb, layer, wblk, lambda r, j: (0, 0, j))],
        out_specs=out_specs,
        out_shape=out_shape,
        compiler_params=pltpu.CompilerParams(
            dimension_semantics=("arbitrary", "arbitrary"), vmem_limit_bytes=_vmem_limit(nbytes)),
        name="branch_merge",
    )(y, z, z, z, z, wb)
    return out if cast else out[0]


def _layernorm_rows(x, g, b):
    mu = jnp.mean(x, axis=-1, keepdims=True)
    xc = x - mu
    var = jnp.mean(xc * xc, axis=-1, keepdims=True)
    return xc * lax.rsqrt(var + EPS) * g + b


def _mix_reset(first, exts, state_refs):
    @pl.when(first)
    def _():
        for idx, ext in enumerate(exts):
            if state_refs is None:
                ext[0:HALO, :] = jnp.zeros((HALO, D_BRANCH), F32)
            else:
                ext[0:HALO, :] = state_refs[idx][0]


def _mix_stage(z_ref, exts, tt, rc):
    pext, scext, ccext = exts
    C = D_BRANCH
    for r in range(0, tt, rc):
        rows = pl.ds(r, rc)
        dst = pl.ds(HALO + r, rc)
        pext[dst, :] = z_ref[rows, OFF_POOL:OFF_POOL + C].astype(F32)
        h = z_ref[rows, OFF_SC:OFF_SC + C].astype(F32)
        cg = z_ref[rows, OFF_SC + 2 * C:OFF_SC + 3 * C].astype(F32)
        scext[dst, :] = cg * h
        a = z_ref[rows, OFF_CC:OFF_CC + C].astype(F32)
        g = z_ref[rows, OFF_CC + C:OFF_CC + 2 * C].astype(F32)
        ccext[dst, :] = a * jax.nn.sigmoid(g)


def _mix_pool(pext, dst_ref, col0, pos0, tt, rc):
    for r in range(0, tt, rc):
        pos = pos0 + r + lax.broadcasted_iota(jnp.int32, (rc, POOL_GROUP), 0)
        for gi, w in enumerate(POOL_WINDOWS):
            lanes = slice(gi * POOL_GROUP, (gi + 1) * POOL_GROUP)
            tok = pext[pl.ds(HALO + r, rc), lanes]
            tot = tok
            for k in range(1, w):
                tot = tot + pext[pl.ds(HALO + r - k, rc), lanes]
            cnt = jnp.minimum(pos + 1, w).astype(F32)
            dst_ref[pl.ds(r, rc), col0 + gi * POOL_GROUP:col0 + (gi + 1) * POOL_GROUP] = (
                tot / cnt - tok).astype(BF16)


def _mix_pool_proj(src_ref, scol0, wpool_ref, pscale_ref, dst_ref, dcol0):
    for gi in range(len(POOL_WINDOWS)):
        lo, hi = gi * POOL_GROUP, (gi + 1) * POOL_GROUP
        ya = jnp.dot(src_ref[:, scol0 + lo:scol0 + hi], wpool_ref[gi].astype(BF16),
                     preferred_element_type=F32)
        dst_ref[:, dcol0 + lo:dcol0 + hi] = (ya * pscale_ref[:, lo:hi]).astype(BF16)


def _mix_sconv(z_ref, scext, wsc_ref, dst_ref, col0, tt, rc):
    C = D_BRANCH
    for r in range(0, tt, rc):
        rows = pl.ds(r, rc)
        acc = None
        for k in range(SCONV_W):
            t = wsc_ref[k:k + 1, :] * scext[pl.ds(HALO + r - (SCONV_W - 1) + k, rc), :]
            acc = t if acc is None else acc + t
        bg = z_ref[rows, OFF_SC + C:OFF_SC + 2 * C].astype(F32)
        dst_ref[rows, col0:col0 + C] = (bg * acc).astype(BF16)


def _mix_cconv(ccext, ccsh, wcc_ref, bcc_ref, gcln_ref, bcln_ref, dst_ref, col0, tt, rc):
    C = D_BRANCH
    sh_rows = ccsh.shape[1]
    for b in range(1, V7X_SUBLANES):
        for r in range(0, sh_rows, rc):
            n = min(rc, sh_rows - r)
            ccsh[b - 1, pl.ds(r, n), :] = ccext[pl.ds(r + b, n), :]
    for r in range(0, tt, rc):
        acc = None
        for k in range(CCONV_W):
            o = k + HALO - (CCONV_W - 1)
            o8, ob = (o // V7X_SUBLANES) * V7X_SUBLANES, o % V7X_SUBLANES
            src = ccext[pl.ds(r + o8, rc), :] if ob == 0 else ccsh[ob - 1, pl.ds(r + o8, rc), :]
            t = wcc_ref[k:k + 1, :] * src
            acc = t if acc is None else acc + t
        yl = _layernorm_rows(acc + bcc_ref[...], gcln_ref[...], bcln_ref[...])
        dst_ref[pl.ds(r, rc), col0:col0 + C] = (yl * jax.nn.sigmoid(yl)).astype(BF16)


def _mix_v(z_ref, gvln_ref, bvln_ref, dst_ref, col0, v_ref, tt, rc):
    C = D_BRANCH
    for r in range(0, tt, rc):
        rows = pl.ds(r, rc)
        v = _layernorm_rows(z_ref[rows, OFF_GM + C:OFF_GM + 2 * C].astype(F32),
                            gvln_ref[...], bvln_ref[...])
        if v_ref is not None:
            v_ref[rows, :] = v
        dst_ref[rows, col0:col0 + C] = v.astype(BF16)


def _mix_gmlp(vb_ref, vcol0, u_ref, ucol0, ws_ref, bs_ref, dst_ref, dcol0, tt, chunk):
    ri = lax.broadcasted_iota(jnp.int32, (chunk, chunk), 0)
    ci = lax.broadcasted_iota(jnp.int32, (chunk, chunk), 1)
    for hh in range(N_GMLP_HEADS):
        lo, hi = hh * GMLP_HEAD, (hh + 1) * GMLP_HEAD
        wsm = jnp.where(ci <= ri, ws_ref[hh], 0.0).astype(BF16)
        for n in range(tt // chunk):
            rows = pl.ds(n * chunk, chunk)
            mixed = (jnp.dot(wsm, vb_ref[rows, vcol0 + lo:vcol0 + hi], preferred_element_type=F32)
                     + bs_ref[:, lo:hi])
            u = u_ref[rows, ucol0 + lo:ucol0 + hi].astype(F32)
            dst_ref[rows, dcol0 + lo:dcol0 + hi] = (u * mixed).astype(BF16)


def _mix_tail(exts, outs, tt):
    for ext, out in zip(exts, outs):
        tail = ext[tt:tt + HALO, :]
        ext[0:HALO, :] = tail
        out[0] = tail


def _mixer_kernel(*refs, tt, chunk, start_pos, has_state, emit_v):
    refs = list(refs)
    z_ref = refs.pop(0)
    state_refs = (refs.pop(0), refs.pop(0), refs.pop(0)) if has_state else None
    (wpool_ref, pscale_ref, wsc_ref, wcc_ref, bcc_ref, gcln_ref, bcln_ref, gvln_ref, bvln_ref,
     ws_ref, bs_ref) = refs[:11]
    refs = refs[11:]
    y_ref, npool_ref, nsc_ref, ncc_ref = refs[:4]
    refs = refs[4:]
    v_ref = refs.pop(0) if emit_v else None
    pext, scext, ccext, ccsh, pooled_s, vb_s = refs

    s = pl.program_id(1)
    C = D_BRANCH
    rc = min(MIX_RC, tt)
    exts = (pext, scext, ccext)

    _mix_reset(s == 0, exts, state_refs)
    _mix_stage(z_ref, exts, tt, rc)
    _mix_pool(pext, pooled_s, 0, start_pos + s * tt, tt, rc)
    _mix_pool_proj(pooled_s, 0, wpool_ref, pscale_ref, y_ref, 0)
    _mix_sconv(z_ref, scext, wsc_ref, y_ref, C, tt, rc)
    _mix_cconv(ccext, ccsh, wcc_ref, bcc_ref, gcln_ref, bcln_ref, y_ref, 2 * C, tt, rc)
    _mix_v(z_ref, gvln_ref, bvln_ref, vb_s, 0, v_ref, tt, rc)
    _mix_gmlp(vb_s, 0, z_ref, OFF_GM, ws_ref, bs_ref, y_ref, 3 * C, tt, chunk)
    _mix_tail(exts, (npool_ref, nsc_ref, ncc_ref), tt)


def _mixers(z, *, n_seq, t, tt, start_pos, states, lw, emit_v):
    m = z.shape[0]
    C = D_BRANCH
    assert m == n_seq * t and t % tt == 0 and tt >= HALO
    assert HALO >= CCONV_W - 1 and HALO % V7X_SUBLANES == 0 and tt % V7X_SUBLANES == 0
    chunk = min(t, GMLP_CHUNK)
    assert tt % chunk == 0
    nt = t // tt
    has_state = states is not None
    sh_rows = tt + HALO - V7X_SUBLANES

    def row_map(b, s):
        return (b * nt + s, 0)

    def const2(b, s):
        return (0, 0)

    def const3(b, s):
        return (0, 0, 0)

    def seq3(b, s):
        return (b, 0, 0)

    in_specs = [pl.BlockSpec((tt, MIX_COLS), row_map)]
    args = [z]
    if has_state:
        for st in states:
            in_specs.append(pl.BlockSpec((1, HALO, C), seq3))
            args.append(st)
    ws = lw["w_spatial"][:, :chunk, :chunk]
    bs = jnp.repeat(lw["b_spatial"][:, :chunk].T, GMLP_HEAD, axis=1)
    small = [
        (lw["w_pool"], const3), (lw["pool_scale"].reshape(1, C), const2),
        (lw["w_sconv"], const2), (lw["w_cconv"], const2), (lw["b_cconv"].reshape(1, C), const2),
        (lw["g_cln"].reshape(1, C), const2), (lw["b_cln"].reshape(1, C), const2),
        (lw["g_vln"].reshape(1, C), const2), (lw["b_vln"].reshape(1, C), const2),
        (ws, const3), (bs, const2),
    ]
    for arr, imap in small:
        in_specs.append(pl.BlockSpec(arr.shape, imap))
        args.append(arr)

    out_shape = [jax.ShapeDtypeStruct((m, D_MODEL), BF16)] + [
        jax.ShapeDtypeStruct((n_seq, HALO, C), F32)] * 3
    out_specs = [pl.BlockSpec((tt, D_MODEL), row_map)] + [pl.BlockSpec((1, HALO, C), seq3)] * 3
    if emit_v:
        out_shape.append(jax.ShapeDtypeStruct((m, C), F32))
        out_specs.append(pl.BlockSpec((tt, C), row_map))

    nbytes = (2 * (tt * MIX_COLS * 2 + tt * D_MODEL * 2 + tt * C * 4)
              + 3 * (tt + HALO) * C * 4 + (V7X_SUBLANES - 1) * sh_rows * C * 4 + 2 * tt * C * 2
              + 16 * HALO * C * 4
              + 16 * tt * C * 4)
    kern = functools.partial(_mixer_kernel, tt=tt, chunk=chunk, start_pos=start_pos,
                             has_state=has_state, emit_v=emit_v)
    return pl.pallas_call(
        kern,
        grid=(n_seq, nt),
        in_specs=in_specs,
        out_specs=out_specs,
        out_shape=out_shape,
        scratch_shapes=[pltpu.VMEM((HALO + tt, C), F32)] * 3
        + [pltpu.VMEM((V7X_SUBLANES - 1, sh_rows, C), F32)]
        + [pltpu.VMEM((tt, C), BF16)] * 2,
        compiler_params=pltpu.CompilerParams(
            dimension_semantics=("arbitrary", "arbitrary"), vmem_limit_bytes=_vmem_limit(nbytes)),
        name="mixers_state" if has_state else "mixers",
    )(*args)


def _pad_state(st):
    return jnp.pad(st, ((0, 0), (HALO - st.shape[1], 0), (0, 0)))


def _layer(x, xb, ss, l, w, lw, g_next, *, n_seq, t, tt, start_pos, states, emit_v):
    m = x.shape[0]
    tm = _row_tile(m, TM_MAX)
    cast = w["in"].dtype != BF16
    wb = {}

    def split(out, n_main):
        out = list(out) if isinstance(out, (list, tuple)) else [out]
        return (out[:n_main], out[n_main] if cast else None)

    tn_wide = 1024 if not cast else 512
    (z,), wb["in"] = split(_matmul_norm(xb, w["in"], l, ss, tm=tm, tn=tn_wide, name="in_proj"), 1)
    mix = _mixers(z, n_seq=n_seq, t=t, tt=tt, start_pos=start_pos, states=states, lw=lw,
                  emit_v=emit_v)
    y, new_states, v = mix[0], mix[1:4], (mix[4] if emit_v else None)
    (merged,), wb["branch"] = split(_branch_merge(y, z, w["branch"], l, tm=tm, tn=512), 1)
    (x, xb, ss), wb["out"] = split(_matmul_res_emit(merged, w["out"], l, x, lw["g_ffn"], tm=tm,
                                                     tn=512, name="out_proj"), 3)
    (hid,), wb["ff1"] = split(_matmul_norm(xb, w["ff1"], l, ss, tm=tm, tn=tn_wide, act="relu2",
                                           name="ff1"), 1)
    n_main = 1 if g_next is None else 3
    main, wb["ff2"] = split(_matmul_kacc(hid, w["ff2"], l, x, g_next, tm=tm, tn=1024,
                                         tk=2048 if cast else 4096,
                                         name="ff2" if g_next is not None else "ff2_last"), n_main)
    x, xb, ss = (main[0], None, None) if g_next is None else main
    return x, xb, ss, new_states, v, wb


def kernel(x_prompt, x_sample, state_pool, state_sconv, state_cconv, g_mix, w_in, w_pool,
           pool_scale, w_sconv, w_cconv, b_cconv, g_cln, b_cln, g_vln, b_vln, w_spatial,
           b_spatial, w_branch, w_out, g_ffn, w_ff1, w_ff2, g_final):
    bp, tp, d = x_prompt.shape
    bs_, ts, _ = x_sample.shape
    assert d == D_MODEL and ts <= 64
    mp, ms = bp * tp, bs_ * ts
    tt_p = 256
    tt_s = ts

    xp, xs = x_prompt.reshape(mp, d), x_sample.reshape(ms, d)
    xbp, ssp = _prep(xp, g_mix[0], tr=_row_tile(mp, TR_MAX))
    xbs, sss = _prep(xs, g_mix[0], tr=_row_tile(ms, TR_MAX))

    w_f32 = {"in": w_in, "branch": w_branch, "out": w_out, "ff1": w_ff1, "ff2": w_ff2}
    outs_p, outs_s, vs = [], [], []
    for l in range(DEPTH):
        lw = dict(w_pool=w_pool[l], pool_scale=pool_scale[l], w_sconv=w_sconv[l],
                  w_cconv=w_cconv[l], b_cconv=b_cconv[l], g_cln=g_cln[l], b_cln=b_cln[l],
                  g_vln=g_vln[l], b_vln=b_vln[l], w_spatial=w_spatial[l], b_spatial=b_spatial[l],
                  g_ffn=g_ffn[l])
        g_next = g_mix[l + 1] if l + 1 < DEPTH else None
        states = (_pad_state(state_pool[l]), _pad_state(state_sconv[l]), _pad_state(state_cconv[l]))
        xs, xbs, sss, st_s, v, w_b16 = _layer(xs, xbs, sss, l, w_f32, lw, g_next, n_seq=bs_, t=ts,
                                              tt=tt_s, start_pos=PAST_LEN, states=states,
                                              emit_v=True)
        xp, xbp, ssp, st_p, _, _ = _layer(xp, xbp, ssp, l, w_b16, lw, g_next, n_seq=bp, t=tp,
                                          tt=tt_p, start_pos=0, states=None, emit_v=False)
        outs_p.append(st_p)
        outs_s.append(st_s)
        vs.append(v.reshape(bs_, ts, D_BRANCH))

    y_prompt = _rmsnorm(xp, g_final, tr=_row_tile(mp, TR_MAX))
    y_sample = _rmsnorm(xs, g_final, tr=_row_tile(ms, TR_MAX))

    def stack(outs, idx, keep):
        return jnp.stack([o[idx][:, HALO - keep:] for o in outs])

    return (y_prompt.reshape(bp, tp, d), y_sample.reshape(bs_, ts, d),
            stack(outs_p, 0, POOL_STATE), stack(outs_s, 0, POOL_STATE),
            stack(outs_p, 1, SCONV_W - 1), stack(outs_s, 1, SCONV_W - 1),
            stack(outs_p, 2, CCONV_W - 1), stack(outs_s, 2, CCONV_W - 1),
            jnp.stack(vs))
```

```python
import functools

import jax
import jax.numpy as jnp
from jax import lax
from jax.experimental import pallas as pl
from jax.experimental.pallas import tpu as pltpu

F32 = jnp.float32
BF16 = jnp.bfloat16

D_MODEL = 4096
DEPTH = 2
PAST_LEN = 2048
N_BRANCH = 4
D_BRANCH = D_MODEL // N_BRANCH
POOL_WINDOWS = (2, 4, 8, 16)
POOL_GROUP = D_BRANCH // len(POOL_WINDOWS)
POOL_STATE = max(POOL_WINDOWS) - 1
SCONV_W = 3
CCONV_W = 31
GMLP_CHUNK = 128
N_GMLP_HEADS = 8
GMLP_HEAD = D_BRANCH // N_GMLP_HEADS
D_FF = 4 * D_MODEL
EPS = 1e-6
OFF_POOL = 0
OFF_SC = OFF_POOL + D_BRANCH
OFF_CC = OFF_SC + 3 * D_BRANCH
OFF_GM = OFF_CC + 2 * D_BRANCH
OFF_GATE = OFF_GM + 2 * D_BRANCH
IN_COLS = OFF_GATE + N_BRANCH * D_MODEL
MIX_COLS = OFF_GATE

V7X_VMEM_BYTES = 64 * 1024 * 1024
V7X_LANES = 128
V7X_SUBLANES = 8
V7X_SUBLANES_BF16 = 16
MiB = 1024 * 1024

HALO = 32
MIX_RC = 32
TM_MAX = 1024
TR_MAX = 256
SIDE_COLS = 16


def _vmem_limit(nbytes):
    return int(min(V7X_VMEM_BYTES - 2 * MiB, nbytes + 12 * MiB))


def _row_tile(m, target):
    best = None
    for t in range(V7X_SUBLANES_BF16, target + 1, V7X_SUBLANES_BF16):
        if m % t == 0:
            best = t
    assert best is not None, (m, target)
    return best


def _fold_lanes(v):
    acc = v[:, 0:V7X_LANES]
    for c in range(1, v.shape[1] // V7X_LANES):
        acc = acc + v[:, c * V7X_LANES:(c + 1) * V7X_LANES]
    return acc


def _row_scale(ss_ref):
    return lax.rsqrt(jnp.sum(ss_ref[...], axis=-1, keepdims=True) * (1.0 / D_MODEL) + EPS)


def _emit_stream(x, g_ref, xb_ref, ss_ref, first):
    xb_ref[...] = (x * g_ref[...]).astype(BF16)
    part = _fold_lanes(x * x)

    @pl.when(first)
    def _():
        ss_ref[...] = part

    @pl.when(jnp.logical_not(first))
    def _():
        ss_ref[...] += part


def _weight_spec(w, layer, blk, idx):
    if w.dtype == BF16:
        return pl.BlockSpec(blk, idx)
    return pl.BlockSpec((None,) + blk, lambda *g: (layer,) + idx(*g))


def _load_weight(w_ref, wb_ref):
    w = w_ref[...]
    if wb_ref is not None:
        w = w.astype(BF16)
        wb_ref[...] = w
    return w


def _prep_kernel(x_ref, g_ref, xb_ref, ss_ref):
    x = x_ref[...]
    xb_ref[...] = (x * g_ref[...]).astype(BF16)
    ss_ref[...] = _fold_lanes(x * x)


def _prep(x, g, *, tr):
    m, d = x.shape
    assert m % tr == 0
    nbytes = 2 * tr * d * (4 + 2) + 2 * tr * d * 4
    return pl.pallas_call(
        _prep_kernel,
        grid=(m // tr,),
        in_specs=[pl.BlockSpec((tr, d), lambda i: (i, 0)),
                  pl.BlockSpec((1, d), lambda i: (0, 0))],
        out_specs=[pl.BlockSpec((tr, d), lambda i: (i, 0)),
                   pl.BlockSpec((tr, V7X_LANES), lambda i: (i, 0))],
        out_shape=[jax.ShapeDtypeStruct((m, d), BF16), jax.ShapeDtypeStruct((m, V7X_LANES), F32)],
        compiler_params=pltpu.CompilerParams(
            dimension_semantics=("arbitrary",), vmem_limit_bytes=_vmem_limit(nbytes)),
        name="prep",
    )(x, g.reshape(1, d))


def _rmsnorm_kernel(x_ref, g_ref, o_ref):
    x = x_ref[...]
    ms = jnp.mean(x * x, axis=-1, keepdims=True)
    o_ref[...] = (x * lax.rsqrt(ms + EPS) * g_ref[...]).astype(o_ref.dtype)


def _rmsnorm(x, g, *, tr):
    m, d = x.shape
    assert m % tr == 0
    nbytes = 2 * tr * d * (4 + 4) + 2 * tr * d * 4
    return pl.pallas_call(
        _rmsnorm_kernel,
        grid=(m // tr,),
        in_specs=[pl.BlockSpec((tr, d), lambda i: (i, 0)),
                  pl.BlockSpec((1, d), lambda i: (0, 0))],
        out_specs=pl.BlockSpec((tr, d), lambda i: (i, 0)),
        out_shape=jax.ShapeDtypeStruct((m, d), F32),
        compiler_params=pltpu.CompilerParams(
            dimension_semantics=("arbitrary",), vmem_limit_bytes=_vmem_limit(nbytes)),
        name="rmsnorm",
    )(x, g.reshape(1, d))


def _mm_norm_kernel(*refs, act, cast, n_side):
    a_ref, w_ref, ss_ref = refs[:3]
    side_in = refs[3:3 + n_side]
    o_ref = refs[3 + n_side]
    wb_ref = refs[4 + n_side] if cast else None
    side_out = refs[len(refs) - n_side:] if n_side else ()
    for s_in, s_out in zip(side_in, side_out):
        s_out[...] = s_in[...].astype(BF16)
    acc = jnp.dot(a_ref[...], _load_weight(w_ref, wb_ref), preferred_element_type=F32)
    acc = acc * _row_scale(ss_ref)
    if act == "relu2":
        acc = jnp.square(jnp.maximum(acc, 0.0))
    o_ref[...] = acc.astype(o_ref.dtype)


def _matmul_norm(a, w, layer, ss, *, tm, tn, act=None, name, side=()):
    m, k = a.shape
    k2, n = w.shape[-2:]
    cast = w.dtype != BF16
    assert k == k2 and m % tm == 0 and n % tn == 0 and (not cast or m == tm)
    ni, nj = m // tm, n // tn
    wsz = 4 if cast else 2
    nbytes = (2 * (tm * k * 2 + k * tn * wsz + tm * tn * 2 + tm * V7X_LANES * 4) + tm * tn * 4
              + (3 * k * tn * 2 if cast else 0))
    in_specs = [pl.BlockSpec((tm, k), lambda i, j: (i, 0)),
                _weight_spec(w, layer, (k, tn), lambda i, j: (0, j)),
                pl.BlockSpec((tm, V7X_LANES), lambda i, j: (i, 0))]
    args = [a, w, ss]
    out_specs = [pl.BlockSpec((tm, tn), lambda i, j: (i, j))]
    out_shape = [jax.ShapeDtypeStruct((m, n), BF16)]
    if cast:
        out_specs.append(pl.BlockSpec((k, tn), lambda i, j: (0, j)))
        out_shape.append(jax.ShapeDtypeStruct((k, n), BF16))
    side_cols = min(nj, SIDE_COLS)
    for sw, sl in side:
        _, r, c = sw.shape
        assert r % (ni * side_cols) == 0
        rb = r // (ni * side_cols)
        assert rb % V7X_SUBLANES_BF16 == 0
        in_specs.append(pl.BlockSpec(
            (None, rb, c), lambda i, j, sl=sl: (sl, i * side_cols + jnp.minimum(j, side_cols - 1), 0)))
        out_specs.append(pl.BlockSpec(
            (rb, c), lambda i, j: (i * side_cols + jnp.minimum(j, side_cols - 1), 0)))
        out_shape.append(jax.ShapeDtypeStruct((r, c), BF16))
        args.append(sw)
        nbytes += 2 * rb * c * (4 + 2)
    out = pl.pallas_call(
        functools.partial(_mm_norm_kernel, act=act, cast=cast, n_side=len(side)),
        grid=(ni, nj),
        in_specs=in_specs,
        out_specs=out_specs,
        out_shape=out_shape,
        compiler_params=pltpu.CompilerParams(
            dimension_semantics=("arbitrary", "arbitrary"), vmem_limit_bytes=_vmem_limit(nbytes)),
        name=name,
    )(*args)
    return out if (cast or side) else out[0]


def _mm_res_emit_kernel(a_ref, w_ref, r_ref, g_ref, o_ref, xb_ref, ss_ref):
    x = r_ref[...] + jnp.dot(a_ref[...], w_ref[...], preferred_element_type=F32)
    o_ref[...] = x
    _emit_stream(x, g_ref, xb_ref, ss_ref, pl.program_id(1) == 0)


def _matmul_res_emit(a, w, res, g, *, tm, tn, name):
    m, k = a.shape
    k2, n = w.shape
    assert k == k2 and m % tm == 0 and n % tn == 0
    nbytes = (2 * (tm * k * 2 + k * tn * 2 + tm * tn * (4 + 4 + 2) + tm * V7X_LANES * 4)
              + 2 * tm * tn * 4)
    return pl.pallas_call(
        _mm_res_emit_kernel,
        grid=(m // tm, n // tn),
        in_specs=[pl.BlockSpec((tm, k), lambda i, j: (i, 0)),
                  pl.BlockSpec((k, tn), lambda i, j: (0, j)),
                  pl.BlockSpec((tm, tn), lambda i, j: (i, j)),
                  pl.BlockSpec((1, tn), lambda i, j: (0, j))],
        out_specs=[pl.BlockSpec((tm, tn), lambda i, j: (i, j)),
                   pl.BlockSpec((tm, tn), lambda i, j: (i, j)),
                   pl.BlockSpec((tm, V7X_LANES), lambda i, j: (i, 0))],
        out_shape=[jax.ShapeDtypeStruct((m, n), F32), jax.ShapeDtypeStruct((m, n), BF16),
                   jax.ShapeDtypeStruct((m, V7X_LANES), F32)],
        compiler_params=pltpu.CompilerParams(
            dimension_semantics=("arbitrary", "arbitrary"), vmem_limit_bytes=_vmem_limit(nbytes)),
        name=name,
    )(a, w, res, g.reshape(1, n))


def _mm_kacc_kernel(a_ref, w_ref, r_ref, *rest, emit):
    rest = list(rest)
    g_ref = rest.pop(0) if emit else None
    o_ref = rest.pop(0)
    xb_ref, ss_ref = (rest.pop(0), rest.pop(0)) if emit else (None, None)
    kk = pl.program_id(2)

    @pl.when(kk == 0)
    def _():
        o_ref[...] = r_ref[...]

    o_ref[...] += jnp.dot(a_ref[...], w_ref[...], preferred_element_type=F32)

    if emit:
        @pl.when(kk == pl.num_programs(2) - 1)
        def _():
            _emit_stream(o_ref[...], g_ref, xb_ref, ss_ref, pl.program_id(1) == 0)


def _matmul_kacc(a, w, res, g, *, tm, tn, tk, name):
    m, k = a.shape
    k2, n = w.shape
    emit = g is not None
    assert k == k2 and m % tm == 0 and n % tn == 0 and k % tk == 0
    nbytes = 2 * (tm * tk * 2 + tk * tn * 2 + 2 * tm * tn * 4) + 2 * tm * tn * 4
    in_specs = [pl.BlockSpec((tm, tk), lambda i, j, kk: (i, kk)),
                pl.BlockSpec((tk, tn), lambda i, j, kk: (kk, j)),
                pl.BlockSpec((tm, tn), lambda i, j, kk: (i, j))]
    args = [a, w, res]
    out_specs = [pl.BlockSpec((tm, tn), lambda i, j, kk: (i, j))]
    out_shape = [jax.ShapeDtypeStruct((m, n), F32)]
    if emit:
        in_specs.append(pl.BlockSpec((1, tn), lambda i, j, kk: (0, j)))
        args.append(g.reshape(1, n))
        out_specs += [pl.BlockSpec((tm, tn), lambda i, j, kk: (i, j)),
                      pl.BlockSpec((tm, V7X_LANES), lambda i, j, kk: (i, 0))]
        out_shape += [jax.ShapeDtypeStruct((m, n), BF16), jax.ShapeDtypeStruct((m, V7X_LANES), F32)]
        nbytes += 2 * (tm * tn * 2 + tm * V7X_LANES * 4)
    return pl.pallas_call(
        functools.partial(_mm_kacc_kernel, emit=emit),
        grid=(m // tm, n // tn, k // tk),
        in_specs=in_specs,
        out_specs=out_specs,
        out_shape=out_shape,
        compiler_params=pltpu.CompilerParams(
            dimension_semantics=("arbitrary", "arbitrary", "arbitrary"),
            vmem_limit_bytes=_vmem_limit(nbytes)),
        name=name,
    )(*args)


def _branch_kernel(y_ref, g0_ref, g1_ref, g2_ref, g3_ref, wb_ref, o_ref):
    acc = None
    for i, g_ref in enumerate((g0_ref, g1_ref, g2_ref, g3_ref)):
        d = jnp.dot(y_ref[:, i * D_BRANCH:(i + 1) * D_BRANCH], wb_ref[i],
                    preferred_element_type=F32)
        t = jax.nn.sigmoid(g_ref[...].astype(F32)) * d
        acc = t if acc is None else acc + t
    o_ref[...] = acc.astype(o_ref.dtype)


def _branch_merge(y, z, wb, *, tm, tn):
    m = y.shape[0]
    assert m % tm == 0 and D_MODEL % tn == 0 and OFF_GATE % tn == 0
    nbytes = (2 * (tm * D_MODEL * 2 + N_BRANCH * tm * tn * 2 + N_BRANCH * D_BRANCH * tn * 2
                   + tm * tn * 2) + 3 * tm * tn * 4)

    def gate_spec(i):
        base = (OFF_GATE + i * D_MODEL) // tn
        return pl.BlockSpec((tm, tn), lambda r, j: (r, base + j))

    return pl.pallas_call(
        _branch_kernel,
        grid=(m // tm, D_MODEL // tn),
        in_specs=[pl.BlockSpec((tm, D_MODEL), lambda r, j: (r, 0))]
        + [gate_spec(i) for i in range(N_BRANCH)]
        + [pl.BlockSpec((N_BRANCH, D_BRANCH, tn), lambda r, j: (0, 0, j))],
        out_specs=pl.BlockSpec((tm, tn), lambda r, j: (r, j)),
        out_shape=jax.ShapeDtypeStruct((m, D_MODEL), BF16),
        compiler_params=pltpu.CompilerParams(
            dimension_semantics=("arbitrary", "arbitrary"), vmem_limit_bytes=_vmem_limit(nbytes)),
        name="branch_merge",
    )(y, z, z, z, z, wb)


def _layernorm_rows(x, g, b):
    mu = jnp.mean(x, axis=-1, keepdims=True)
    xc = x - mu
    var = jnp.mean(xc * xc, axis=-1, keepdims=True)
    return xc * lax.rsqrt(var + EPS) * g + b


def _mix_reset(first, exts, state_refs):
    @pl.when(first)
    def _():
        for idx, ext in enumerate(exts):
            if state_refs is None:
                ext[0:HALO, :] = jnp.zeros((HALO, D_BRANCH), F32)
            else:
                ext[0:HALO, :] = state_refs[idx][0]


def _mix_stage(z_ref, exts, tt, rc):
    pext, scext, ccext = exts
    C = D_BRANCH
    for r in range(0, tt, rc):
        rows = pl.ds(r, rc)
        dst = pl.ds(HALO + r, rc)
        pext[dst, :] = z_ref[rows, OFF_POOL:OFF_POOL + C].astype(F32)
        h = z_ref[rows, OFF_SC:OFF_SC + C].astype(F32)
        cg = z_ref[rows, OFF_SC + 2 * C:OFF_SC + 3 * C].astype(F32)
        scext[dst, :] = cg * h
        a = z_ref[rows, OFF_CC:OFF_CC + C].astype(F32)
        g = z_ref[rows, OFF_CC + C:OFF_CC + 2 * C].astype(F32)
        ccext[dst, :] = a * jax.nn.sigmoid(g)


def _mix_pool(pext, dst_ref, col0, pos0, tt, rc):
    for r in range(0, tt, rc):
        pos = pos0 + r + lax.broadcasted_iota(jnp.int32, (rc, POOL_GROUP), 0)
        for gi, w in enumerate(POOL_WINDOWS):
            lanes = slice(gi * POOL_GROUP, (gi + 1) * POOL_GROUP)
            tok = pext[pl.ds(HALO + r, rc), lanes]
            tot = tok
            for k in range(1, w):
                tot = tot + pext[pl.ds(HALO + r - k, rc), lanes]
            cnt = jnp.minimum(pos + 1, w).astype(F32)
            dst_ref[pl.ds(r, rc), col0 + gi * POOL_GROUP:col0 + (gi + 1) * POOL_GROUP] = (
                tot / cnt - tok).astype(BF16)


def _mix_pool_proj(src_ref, scol0, wpool_ref, pscale_ref, dst_ref, dcol0):
    for gi in range(len(POOL_WINDOWS)):
        lo, hi = gi * POOL_GROUP, (gi + 1) * POOL_GROUP
        ya = jnp.dot(src_ref[:, scol0 + lo:scol0 + hi], wpool_ref[gi].astype(BF16),
                     preferred_element_type=F32)
        dst_ref[:, dcol0 + lo:dcol0 + hi] = (ya * pscale_ref[:, lo:hi]).astype(BF16)


def _mix_sconv(z_ref, scext, wsc_ref, dst_ref, col0, tt, rc):
    C = D_BRANCH
    for r in range(0, tt, rc):
        rows = pl.ds(r, rc)
        acc = None
        for k in range(SCONV_W):
            t = wsc_ref[k:k + 1, :] * scext[pl.ds(HALO + r - (SCONV_W - 1) + k, rc), :]
            acc = t if acc is None else acc + t
        bg = z_ref[rows, OFF_SC + C:OFF_SC + 2 * C].astype(F32)
        dst_ref[rows, col0:col0 + C] = (bg * acc).astype(BF16)


def _mix_cconv(ccext, ccsh, wcc_ref, bcc_ref, gcln_ref, bcln_ref, dst_ref, col0, tt, rc):
    C = D_BRANCH
    sh_rows = ccsh.shape[1]
    for b in range(1, V7X_SUBLANES):
        for r in range(0, sh_rows, rc):
            n = min(rc, sh_rows - r)
            ccsh[b - 1, pl.ds(r, n), :] = ccext[pl.ds(r + b, n), :]
    for r in range(0, tt, rc):
        acc = None
        for k in range(CCONV_W):
            o = k + HALO - (CCONV_W - 1)
            o8, ob = (o // V7X_SUBLANES) * V7X_SUBLANES, o % V7X_SUBLANES
            src = ccext[pl.ds(r + o8, rc), :] if ob == 0 else ccsh[ob - 1, pl.ds(r + o8, rc), :]
            t = wcc_ref[k:k + 1, :] * src
            acc = t if acc is None else acc + t
        yl = _layernorm_rows(acc + bcc_ref[...], gcln_ref[...], bcln_ref[...])
        dst_ref[pl.ds(r, rc), col0:col0 + C] = (yl * jax.nn.sigmoid(yl)).astype(BF16)


def _mix_v(z_ref, gvln_ref, bvln_ref, dst_ref, col0, v_ref, tt, rc):
    C = D_BRANCH
    for r in range(0, tt, rc):
        rows = pl.ds(r, rc)
        v = _layernorm_rows(z_ref[rows, OFF_GM + C:OFF_GM + 2 * C].astype(F32),
                            gvln_ref[...], bvln_ref[...])
        if v_ref is not None:
            v_ref[rows, :] = v
        dst_ref[rows, col0:col0 + C] = v.astype(BF16)


def _mix_gmlp(vb_ref, vcol0, u_ref, ucol0, ws_ref, bs_ref, dst_ref, dcol0, tt, chunk):
    ri = lax.broadcasted_iota(jnp.int32, (chunk, chunk), 0)
    ci = lax.broadcasted_iota(jnp.int32, (chunk, chunk), 1)
    for hh in range(N_GMLP_HEADS):
        lo, hi = hh * GMLP_HEAD, (hh + 1) * GMLP_HEAD
        wsm = jnp.where(ci <= ri, ws_ref[hh], 0.0).astype(BF16)
        for n in range(tt // chunk):
            rows = pl.ds(n * chunk, chunk)
            mixed = (jnp.dot(wsm, vb_ref[rows, vcol0 + lo:vcol0 + hi], preferred_element_type=F32)
                     + bs_ref[:, lo:hi])
            u = u_ref[rows, ucol0 + lo:ucol0 + hi].astype(F32)
            dst_ref[rows, dcol0 + lo:dcol0 + hi] = (u * mixed).astype(BF16)


def _mix_tail(exts, outs, tt):
    for ext, out in zip(exts, outs):
        tail = ext[tt:tt + HALO, :]
        ext[0:HALO, :] = tail
        out[0] = tail


def _mixer_kernel(*refs, tt, chunk, start_pos, has_state, emit_v):
    refs = list(refs)
    z_ref = refs.pop(0)
    state_refs = (refs.pop(0), refs.pop(0), refs.pop(0)) if has_state else None
    (wpool_ref, pscale_ref, wsc_ref, wcc_ref, bcc_ref, gcln_ref, bcln_ref, gvln_ref, bvln_ref,
     ws_ref, bs_ref) = refs[:11]
    refs = refs[11:]
    y_ref, npool_ref, nsc_ref, ncc_ref = refs[:4]
    refs = refs[4:]
    v_ref = refs.pop(0) if emit_v else None
    pext, scext, ccext, ccsh, pooled_s, vb_s = refs

    s = pl.program_id(1)
    C = D_BRANCH
    rc = min(MIX_RC, tt)
    exts = (pext, scext, ccext)

    _mix_reset(s == 0, exts, state_refs)
    _mix_stage(z_ref, exts, tt, rc)
    _mix_pool(pext, pooled_s, 0, start_pos + s * tt, tt, rc)
    _mix_pool_proj(pooled_s, 0, wpool_ref, pscale_ref, y_ref, 0)
    _mix_sconv(z_ref, scext, wsc_ref, y_ref, C, tt, rc)
    _mix_cconv(ccext, ccsh, wcc_ref, bcc_ref, gcln_ref, bcln_ref, y_ref, 2 * C, tt, rc)
    _mix_v(z_ref, gvln_ref, bvln_ref, vb_s, 0, v_ref, tt, rc)
    _mix_gmlp(vb_s, 0, z_ref, OFF_GM, ws_ref, bs_ref, y_ref, 3 * C, tt, chunk)
    _mix_tail(exts, (npool_ref, nsc_ref, ncc_ref), tt)


def _mixers(z, *, n_seq, t, tt, start_pos, states, lw, emit_v):
    m = z.shape[0]
    C = D_BRANCH
    assert m == n_seq * t and t % tt == 0 and tt >= HALO
    assert HALO >= CCONV_W - 1 and HALO % V7X_SUBLANES == 0 and tt % V7X_SUBLANES == 0
    chunk = min(t, GMLP_CHUNK)
    assert tt % chunk == 0
    nt = t // tt
    has_state = states is not None
    sh_rows = tt + HALO - V7X_SUBLANES

    def row_map(b, s):
        return (b * nt + s, 0)

    def const2(b, s):
        return (0, 0)

    def const3(b, s):
        return (0, 0, 0)

    def seq3(b, s):
        return (b, 0, 0)

    in_specs = [pl.BlockSpec((tt, MIX_COLS), row_map)]
    args = [z]
    if has_state:
        for st in states:
            in_specs.append(pl.BlockSpec((1, HALO, C), seq3))
            args.append(st)
    ws = lw["w_spatial"][:, :chunk, :chunk]
    bs = jnp.repeat(lw["b_spatial"][:, :chunk].T, GMLP_HEAD, axis=1)
    small = [
        (lw["w_pool"], const3), (lw["pool_scale"].reshape(1, C), const2),
        (lw["w_sconv"], const2), (lw["w_cconv"], const2), (lw["b_cconv"].reshape(1, C), const2),
        (lw["g_cln"].reshape(1, C), const2), (lw["b_cln"].reshape(1, C), const2),
        (lw["g_vln"].reshape(1, C), const2), (lw["b_vln"].reshape(1, C), const2),
        (ws, const3), (bs, const2),
    ]
    for arr, imap in small:
        in_specs.append(pl.BlockSpec(arr.shape, imap))
        args.append(arr)

    out_shape = [jax.ShapeDtypeStruct((m, D_MODEL), BF16)] + [
        jax.ShapeDtypeStruct((n_seq, HALO, C), F32)] * 3
    out_specs = [pl.BlockSpec((tt, D_MODEL), row_map)] + [pl.BlockSpec((1, HALO, C), seq3)] * 3
    if emit_v:
        out_shape.append(jax.ShapeDtypeStruct((m, C), F32))
        out_specs.append(pl.BlockSpec((tt, C), row_map))

    nbytes = (2 * (tt * MIX_COLS * 2 + tt * D_MODEL * 2 + tt * C * 4)
              + 3 * (tt + HALO) * C * 4 + (V7X_SUBLANES - 1) * sh_rows * C * 4 + 2 * tt * C * 2
              + 16 * HALO * C * 4
              + 16 * tt * C * 4)
    kern = functools.partial(_mixer_kernel, tt=tt, chunk=chunk, start_pos=start_pos,
                             has_state=has_state, emit_v=emit_v)
    return pl.pallas_call(
        kern,
        grid=(n_seq, nt),
        in_specs=in_specs,
        out_specs=out_specs,
        out_shape=out_shape,
        scratch_shapes=[pltpu.VMEM((HALO + tt, C), F32)] * 3
        + [pltpu.VMEM((V7X_SUBLANES - 1, sh_rows, C), F32)]
        + [pltpu.VMEM((tt, C), BF16)] * 2,
        compiler_params=pltpu.CompilerParams(
            dimension_semantics=("arbitrary", "arbitrary"), vmem_limit_bytes=_vmem_limit(nbytes)),
        name="mixers_state" if has_state else "mixers",
    )(*args)


def _pad_state(st):
    return jnp.pad(st, ((0, 0), (HALO - st.shape[1], 0), (0, 0)))


def _after_in_proj(x, z, wb, lw, g_next, *, n_seq, t, tt, start_pos, states, emit_v, ff1_side=()):
    tm = _row_tile(x.shape[0], TM_MAX)
    mix = _mixers(z, n_seq=n_seq, t=t, tt=tt, start_pos=start_pos, states=states, lw=lw,
                  emit_v=emit_v)
    y, new_states, v = mix[0], mix[1:4], (mix[4] if emit_v else None)
    merged = _branch_merge(y, z, wb["branch"], tm=tm, tn=512)
    x, xb, ss = _matmul_res_emit(merged, wb["out"], x, lw["g_ffn"], tm=tm, tn=1024, name="out_proj")
    out = _matmul_norm(xb, wb["ff1"], 0, ss, tm=tm, tn=1024, act="relu2", name="ff1", side=ff1_side)
    hid, casts = (out[0], list(out[1:])) if ff1_side else (out, [])
    main = _matmul_kacc(hid, wb["ff2"], x, g_next, tm=tm, tn=1024, tk=4096,
                        name="ff2" if g_next is not None else "ff2_last")
    x, xb, ss = (main[0], None, None) if g_next is None else main
    return x, xb, ss, new_states, v, casts


def kernel(x_prompt, x_sample, state_pool, state_sconv, state_cconv, g_mix, w_in, w_pool,
           pool_scale, w_sconv, w_cconv, b_cconv, g_cln, b_cln, g_vln, b_vln, w_spatial,
           b_spatial, w_branch, w_out, g_ffn, w_ff1, w_ff2, g_final):
    bp, tp, d = x_prompt.shape
    bs_, ts, _ = x_sample.shape
    assert d == D_MODEL and ts <= 64
    mp, ms = bp * tp, bs_ * ts
    tt_p = 256
    tt_s = ts

    xp, xs = x_prompt.reshape(mp, d), x_sample.reshape(ms, d)
    xbp, ssp = _prep(xp, g_mix[0], tr=_row_tile(mp, TR_MAX))
    xbs, sss = _prep(xs, g_mix[0], tr=_row_tile(ms, TR_MAX))

    tm_p, tm_s = _row_tile(mp, TM_MAX), _row_tile(ms, TM_MAX)
    w3 = {"in": w_in, "branch": w_branch.reshape(DEPTH, N_BRANCH * D_BRANCH, D_MODEL),
          "out": w_out, "ff1": w_ff1, "ff2": w_ff2}

    def store(wb_l, keys, casts):
        for key, c in zip(keys, casts):
            wb_l[key] = c.reshape(N_BRANCH, D_BRANCH, D_MODEL) if key == "branch" else c

    wb = [dict() for _ in range(DEPTH)]
    z_s, wb[0]["in"] = _matmul_norm(xbs, w_in, 0, sss, tm=tm_s, tn=768, name="in_proj")
    keys0 = ["branch", "out", "ff1", "ff2"]
    side0 = [(w3[k], 0) for k in keys0] + ([(w3["ff2"], 1)] if DEPTH > 1 else [])
    out = _matmul_norm(xbp, wb[0]["in"], 0, ssp, tm=tm_p, tn=1024, name="in_proj", side=side0)
    z_p = out[0]
    store(wb[0], keys0, out[1:1 + len(keys0)])
    if DEPTH > 1:
        wb[1]["ff2"] = out[1 + len(keys0)]

    outs_p, outs_s, vs = [], [], []
    for l in range(DEPTH):
        lw = dict(w_pool=w_pool[l], pool_scale=pool_scale[l], w_sconv=w_sconv[l],
                  w_cconv=w_cconv[l], b_cconv=b_cconv[l], g_cln=g_cln[l], b_cln=b_cln[l],
                  g_vln=g_vln[l], b_vln=b_vln[l], w_spatial=w_spatial[l], b_spatial=b_spatial[l],
                  g_ffn=g_ffn[l])
        last = l + 1 == DEPTH
        g_next = None if last else g_mix[l + 1]
        states = (_pad_state(state_pool[l]), _pad_state(state_sconv[l]), _pad_state(state_cconv[l]))
        keys_next = [] if last else ["in", "branch", "out", "ff1"] + (["ff2"] if l + 1 > 1 else [])
        xs, xbs, sss, st_s, v, _ = _after_in_proj(
            xs, z_s, wb[l], lw, g_next, n_seq=bs_, t=ts, tt=tt_s, start_pos=PAST_LEN, states=states,
            emit_v=True)
        xp, xbp, ssp, st_p, _, casts = _after_in_proj(
            xp, z_p, wb[l], lw, g_next, n_seq=bp, t=tp, tt=tt_p, start_pos=0, states=None,
            emit_v=False, ff1_side=[(w3[k], l + 1) for k in keys_next])
        outs_p.append(st_p)
        outs_s.append(st_s)
        vs.append(v.reshape(bs_, ts, D_BRANCH))
        if not last:
            store(wb[l + 1], keys_next, casts)
            z_s = _matmul_norm(xbs, wb[l + 1]["in"], 0, sss, tm=tm_s, tn=1024, name="in_proj")
            z_p = _matmul_norm(xbp, wb[l + 1]["in"], 0, ssp, tm=tm_p, tn=1024, name="in_proj")

    y_prompt = _rmsnorm(xp, g_final, tr=_row_tile(mp, TR_MAX))
    y_sample = _rmsnorm(xs, g_final, tr=_row_tile(ms, TR_MAX))

    def stack(outs, idx, keep):
        return jnp.stack([o[idx][:, HALO - keep:] for o in outs])

    return (y_prompt.reshape(bp, tp, d), y_sample.reshape(bs_, ts, d),
            stack(outs_p, 0, POOL_STATE), stack(outs_s, 0, POOL_STATE),
            stack(outs_p, 1, SCONV_W - 1), stack(outs_s, 1, SCONV_W - 1),
            stack(outs_p, 2, CCONV_W - 1), stack(outs_s, 2, CCONV_W - 1),
            jnp.stack(vs))
```

```python
import functools

import jax
import jax.numpy as jnp
from jax import lax
from jax.experimental import pallas as pl
from jax.experimental.pallas import tpu as pltpu

F32 = jnp.float32
BF16 = jnp.bfloat16

D_MODEL = 4096
DEPTH = 2
PAST_LEN = 2048
N_BRANCH = 4
D_BRANCH = D_MODEL // N_BRANCH
POOL_WINDOWS = (2, 4, 8, 16)
POOL_GROUP = D_BRANCH // len(POOL_WINDOWS)
POOL_STATE = max(POOL_WINDOWS) - 1
SCONV_W = 3
CCONV_W = 31
GMLP_CHUNK = 128
N_GMLP_HEADS = 8
GMLP_HEAD = D_BRANCH // N_GMLP_HEADS
D_FF = 4 * D_MODEL
EPS = 1e-6
OFF_POOL = 0
OFF_SC = OFF_POOL + D_BRANCH
OFF_CC = OFF_SC + 3 * D_BRANCH
OFF_GM = OFF_CC + 2 * D_BRANCH
OFF_GATE = OFF_GM + 2 * D_BRANCH
IN_COLS = OFF_GATE + N_BRANCH * D_MODEL
MIX_COLS = OFF_GATE

V7X_VMEM_BYTES = 64 * 1024 * 1024
V7X_LANES = 128
V7X_SUBLANES = 8
V7X_SUBLANES_BF16 = 16
MiB = 1024 * 1024

HALO = 32
MIX_RC = 32
CCONV_LANES = 128
TM_MAX = 1024
TR_MAX = 256
SIDE_COLS = 16


def _vmem_limit(nbytes):
    return int(min(V7X_VMEM_BYTES - 2 * MiB, nbytes + 12 * MiB))


def _row_tile(m, target):
    best = None
    for t in range(V7X_SUBLANES_BF16, target + 1, V7X_SUBLANES_BF16):
        if m % t == 0:
            best = t
    assert best is not None, (m, target)
    return best


def _fold_lanes(v):
    acc = v[:, 0:V7X_LANES]
    for c in range(1, v.shape[1] // V7X_LANES):
        acc = acc + v[:, c * V7X_LANES:(c + 1) * V7X_LANES]
    return acc


def _row_scale(ss_ref):
    return lax.rsqrt(jnp.sum(ss_ref[...], axis=-1, keepdims=True) * (1.0 / D_MODEL) + EPS)


def _emit_stream(x, g_ref, xb_ref, ss_ref, first):
    xb_ref[...] = (x * g_ref[...]).astype(BF16)
    part = _fold_lanes(x * x)

    @pl.when(first)
    def _():
        ss_ref[...] = part

    @pl.when(jnp.logical_not(first))
    def _():
        ss_ref[...] += part


def _weight_spec(w, layer, blk, idx):
    if w.dtype == BF16:
        return pl.BlockSpec(blk, idx)
    return pl.BlockSpec((None,) + blk, lambda *g: (layer,) + idx(*g))


def _load_weight(w_ref, wb_ref):
    w = w_ref[...]
    if wb_ref is not None:
        w = w.astype(BF16)
        wb_ref[...] = w
    return w


def _prep_kernel(x_ref, g_ref, xb_ref, ss_ref):
    x = x_ref[...]
    xb_ref[...] = (x * g_ref[...]).astype(BF16)
    ss_ref[...] = _fold_lanes(x * x)


def _prep(x, g, *, tr):
    m, d = x.shape
    assert m % tr == 0
    nbytes = 2 * tr * d * (4 + 2) + 2 * tr * d * 4
    return pl.pallas_call(
        _prep_kernel,
        grid=(m // tr,),
        in_specs=[pl.BlockSpec((tr, d), lambda i: (i, 0)),
                  pl.BlockSpec((1, d), lambda i: (0, 0))],
        out_specs=[pl.BlockSpec((tr, d), lambda i: (i, 0)),
                   pl.BlockSpec((tr, V7X_LANES), lambda i: (i, 0))],
        out_shape=[jax.ShapeDtypeStruct((m, d), BF16), jax.ShapeDtypeStruct((m, V7X_LANES), F32)],
        compiler_params=pltpu.CompilerParams(
            dimension_semantics=("arbitrary",), vmem_limit_bytes=_vmem_limit(nbytes)),
        name="prep",
    )(x, g.reshape(1, d))


def _rmsnorm_kernel(x_ref, g_ref, o_ref):
    x = x_ref[...]
    ms = jnp.mean(x * x, axis=-1, keepdims=True)
    o_ref[...] = (x * lax.rsqrt(ms + EPS) * g_ref[...]).astype(o_ref.dtype)


def _rmsnorm(x, g, *, tr):
    m, d = x.shape
    assert m % tr == 0
    nbytes = 2 * tr * d * (4 + 4) + 2 * tr * d * 4
    return pl.pallas_call(
        _rmsnorm_kernel,
        grid=(m // tr,),
        in_specs=[pl.BlockSpec((tr, d), lambda i: (i, 0)),
                  pl.BlockSpec((1, d), lambda i: (0, 0))],
        out_specs=pl.BlockSpec((tr, d), lambda i: (i, 0)),
        out_shape=jax.ShapeDtypeStruct((m, d), F32),
        compiler_params=pltpu.CompilerParams(
            dimension_semantics=("arbitrary",), vmem_limit_bytes=_vmem_limit(nbytes)),
        name="rmsnorm",
    )(x, g.reshape(1, d))


def _mm_norm_kernel(*refs, act, cast, n_side):
    a_ref, w_ref, ss_ref = refs[:3]
    side_in = refs[3:3 + n_side]
    o_ref = refs[3 + n_side]
    wb_ref = refs[4 + n_side] if cast else None
    side_out = refs[len(refs) - n_side:] if n_side else ()
    for s_in, s_out in zip(side_in, side_out):
        s_out[...] = s_in[...].astype(BF16)
    acc = jnp.dot(a_ref[...], _load_weight(w_ref, wb_ref), preferred_element_type=F32)
    acc = acc * _row_scale(ss_ref)
    if act == "relu2":
        acc = jnp.square(jnp.maximum(acc, 0.0))
    o_ref[...] = acc.astype(o_ref.dtype)


def _matmul_norm(a, w, layer, ss, *, tm, tn, act=None, name, side=()):
    m, k = a.shape
    k2, n = w.shape[-2:]
    cast = w.dtype != BF16
    assert k == k2 and m % tm == 0 and n % tn == 0 and (not cast or m == tm)
    ni, nj = m // tm, n // tn
    wsz = 4 if cast else 2
    nbytes = (2 * (tm * k * 2 + k * tn * wsz + tm * tn * 2 + tm * V7X_LANES * 4) + tm * tn * 4
              + (3 * k * tn * 2 if cast else 0))
    in_specs = [pl.BlockSpec((tm, k), lambda i, j: (i, 0)),
                _weight_spec(w, layer, (k, tn), lambda i, j: (0, j)),
                pl.BlockSpec((tm, V7X_LANES), lambda i, j: (i, 0))]
    args = [a, w, ss]
    out_specs = [pl.BlockSpec((tm, tn), lambda i, j: (i, j))]
    out_shape = [jax.ShapeDtypeStruct((m, n), BF16)]
    if cast:
        out_specs.append(pl.BlockSpec((k, tn), lambda i, j: (0, j)))
        out_shape.append(jax.ShapeDtypeStruct((k, n), BF16))
    side_cols = min(nj, SIDE_COLS)
    for sw, sl in side:
        _, r, c = sw.shape
        assert r % (ni * side_cols) == 0
        rb = r // (ni * side_cols)
        assert rb % V7X_SUBLANES_BF16 == 0
        in_specs.append(pl.BlockSpec(
            (None, rb, c), lambda i, j, sl=sl: (sl, i * side_cols + jnp.minimum(j, side_cols - 1), 0)))
        out_specs.append(pl.BlockSpec(
            (rb, c), lambda i, j: (i * side_cols + jnp.minimum(j, side_cols - 1), 0)))
        out_shape.append(jax.ShapeDtypeStruct((r, c), BF16))
        args.append(sw)
        nbytes += 2 * rb * c * (4 + 2)
    out = pl.pallas_call(
        functools.partial(_mm_norm_kernel, act=act, cast=cast, n_side=len(side)),
        grid=(ni, nj),
        in_specs=in_specs,
        out_specs=out_specs,
        out_shape=out_shape,
        compiler_params=pltpu.CompilerParams(
            dimension_semantics=("arbitrary", "arbitrary"), vmem_limit_bytes=_vmem_limit(nbytes)),
        name=name,
    )(*args)
    return out if (cast or side) else out[0]


def _mm_res_emit_kernel(a_ref, w_ref, r_ref, g_ref, o_ref, xb_ref, ss_ref):
    x = r_ref[...] + jnp.dot(a_ref[...], w_ref[...], preferred_element_type=F32)
    o_ref[...] = x
    _emit_stream(x, g_ref, xb_ref, ss_ref, pl.program_id(1) == 0)


def _matmul_res_emit(a, w, res, g, *, tm, tn, name):
    m, k = a.shape
    k2, n = w.shape
    assert k == k2 and m % tm == 0 and n % tn == 0
    nbytes = (2 * (tm * k * 2 + k * tn * 2 + tm * tn * (4 + 4 + 2) + tm * V7X_LANES * 4)
              + 2 * tm * tn * 4)
    return pl.pallas_call(
        _mm_res_emit_kernel,
        grid=(m // tm, n // tn),
        in_specs=[pl.BlockSpec((tm, k), lambda i, j: (i, 0)),
                  pl.BlockSpec((k, tn), lambda i, j: (0, j)),
                  pl.BlockSpec((tm, tn), lambda i, j: (i, j)),
                  pl.BlockSpec((1, tn), lambda i, j: (0, j))],
        out_specs=[pl.BlockSpec((tm, tn), lambda i, j: (i, j)),
                   pl.BlockSpec((tm, tn), lambda i, j: (i, j)),
                   pl.BlockSpec((tm, V7X_LANES), lambda i, j: (i, 0))],
        out_shape=[jax.ShapeDtypeStruct((m, n), F32), jax.ShapeDtypeStruct((m, n), BF16),
                   jax.ShapeDtypeStruct((m, V7X_LANES), F32)],
        compiler_params=pltpu.CompilerParams(
            dimension_semantics=("arbitrary", "arbitrary"), vmem_limit_bytes=_vmem_limit(nbytes)),
        name=name,
    )(a, w, res, g.reshape(1, n))


def _mm_kacc_kernel(a_ref, w_ref, r_ref, *rest, emit):
    rest = list(rest)
    g_ref = rest.pop(0) if emit else None
    o_ref = rest.pop(0)
    xb_ref, ss_ref = (rest.pop(0), rest.pop(0)) if emit else (None, None)
    kk = pl.program_id(2)

    @pl.when(kk == 0)
    def _():
        o_ref[...] = r_ref[...]

    o_ref[...] += jnp.dot(a_ref[...], w_ref[...], preferred_element_type=F32)

    if emit:
        @pl.when(kk == pl.num_programs(2) - 1)
        def _():
            _emit_stream(o_ref[...], g_ref, xb_ref, ss_ref, pl.program_id(1) == 0)


def _matmul_kacc(a, w, res, g, *, tm, tn, tk, name):
    m, k = a.shape
    k2, n = w.shape
    emit = g is not None
    assert k == k2 and m % tm == 0 and n % tn == 0 and k % tk == 0
    nbytes = 2 * (tm * tk * 2 + tk * tn * 2 + 2 * tm * tn * 4) + 2 * tm * tn * 4
    in_specs = [pl.BlockSpec((tm, tk), lambda i, j, kk: (i, kk)),
                pl.BlockSpec((tk, tn), lambda i, j, kk: (kk, j)),
                pl.BlockSpec((tm, tn), lambda i, j, kk: (i, j))]
    args = [a, w, res]
    out_specs = [pl.BlockSpec((tm, tn), lambda i, j, kk: (i, j))]
    out_shape = [jax.ShapeDtypeStruct((m, n), F32)]
    if emit:
        in_specs.append(pl.BlockSpec((1, tn), lambda i, j, kk: (0, j)))
        args.append(g.reshape(1, n))
        out_specs += [pl.BlockSpec((tm, tn), lambda i, j, kk: (i, j)),
                      pl.BlockSpec((tm, V7X_LANES), lambda i, j, kk: (i, 0))]
        out_shape += [jax.ShapeDtypeStruct((m, n), BF16), jax.ShapeDtypeStruct((m, V7X_LANES), F32)]
        nbytes += 2 * (tm * tn * 2 + tm * V7X_LANES * 4)
    return pl.pallas_call(
        functools.partial(_mm_kacc_kernel, emit=emit),
        grid=(m // tm, n // tn, k // tk),
        in_specs=in_specs,
        out_specs=out_specs,
        out_shape=out_shape,
        compiler_params=pltpu.CompilerParams(
            dimension_semantics=("arbitrary", "arbitrary", "arbitrary"),
            vmem_limit_bytes=_vmem_limit(nbytes)),
        name=name,
    )(*args)


def _branch_kernel(y_ref, g0_ref, g1_ref, g2_ref, g3_ref, wb_ref, o_ref):
    acc = None
    for i, g_ref in enumerate((g0_ref, g1_ref, g2_ref, g3_ref)):
        d = jnp.dot(y_ref[:, i * D_BRANCH:(i + 1) * D_BRANCH], wb_ref[i],
                    preferred_element_type=F32)
        t = jax.nn.sigmoid(g_ref[...].astype(F32)) * d
        acc = t if acc is None else acc + t
    o_ref[...] = acc.astype(o_ref.dtype)


def _branch_merge(y, z, wb, *, tm, tn):
    m = y.shape[0]
    assert m % tm == 0 and D_MODEL % tn == 0 and OFF_GATE % tn == 0
    nbytes = (2 * (tm * D_MODEL * 2 + N_BRANCH * tm * tn * 2 + N_BRANCH * D_BRANCH * tn * 2
                   + tm * tn * 2) + 3 * tm * tn * 4)

    def gate_spec(i):
        base = (OFF_GATE + i * D_MODEL) // tn
        return pl.BlockSpec((tm, tn), lambda r, j: (r, base + j))

    return pl.pallas_call(
        _branch_kernel,
        grid=(m // tm, D_MODEL // tn),
        in_specs=[pl.BlockSpec((tm, D_MODEL), lambda r, j: (r, 0))]
        + [gate_spec(i) for i in range(N_BRANCH)]
        + [pl.BlockSpec((N_BRANCH, D_BRANCH, tn), lambda r, j: (0, 0, j))],
        out_specs=pl.BlockSpec((tm, tn), lambda r, j: (r, j)),
        out_shape=jax.ShapeDtypeStruct((m, D_MODEL), BF16),
        compiler_params=pltpu.CompilerParams(
            dimension_semantics=("arbitrary", "arbitrary"), vmem_limit_bytes=_vmem_limit(nbytes)),
        name="branch_merge",
    )(y, z, z, z, z, wb)


def _layernorm_rows(x, g, b):
    mu = jnp.mean(x, axis=-1, keepdims=True)
    xc = x - mu
    var = jnp.mean(xc * xc, axis=-1, keepdims=True)
    return xc * lax.rsqrt(var + EPS) * g + b


def _mix_reset(first, exts, state_refs):
    @pl.when(first)
    def _():
        for idx, ext in enumerate(exts):
            if state_refs is None:
                ext[0:HALO, :] = jnp.zeros((HALO, D_BRANCH), F32)
            else:
                ext[0:HALO, :] = state_refs[idx][0]


def _mix_stage(z_ref, exts, tt, rc):
    pext, scext, ccext = exts
    C = D_BRANCH
    for r in range(0, tt, rc):
        rows = pl.ds(r, rc)
        dst = pl.ds(HALO + r, rc)
        for c0 in range(0, C, CCONV_LANES):
            c1 = c0 + CCONV_LANES
            pext[dst, c0:c1] = z_ref[rows, OFF_POOL + c0:OFF_POOL + c1].astype(F32)
            h = z_ref[rows, OFF_SC + c0:OFF_SC + c1].astype(F32)
            cg = z_ref[rows, OFF_SC + 2 * C + c0:OFF_SC + 2 * C + c1].astype(F32)
            scext[dst, c0:c1] = cg * h
            a = z_ref[rows, OFF_CC + c0:OFF_CC + c1].astype(F32)
            g = z_ref[rows, OFF_CC + C + c0:OFF_CC + C + c1].astype(F32)
            ccext[dst, c0:c1] = a * jax.nn.sigmoid(g)


def _mix_pool(pext, dst_ref, col0, pos0, tt, rc):
    for r in range(0, tt, rc):
        pos = pos0 + r + lax.broadcasted_iota(jnp.int32, (rc, POOL_GROUP), 0)
        for gi, w in enumerate(POOL_WINDOWS):
            lanes = slice(gi * POOL_GROUP, (gi + 1) * POOL_GROUP)
            tok = pext[pl.ds(HALO + r, rc), lanes]
            tot = tok
            for k in range(1, w):
                tot = tot + pext[pl.ds(HALO + r - k, rc), lanes]
            cnt = jnp.minimum(pos + 1, w).astype(F32)
            dst_ref[pl.ds(r, rc), col0 + gi * POOL_GROUP:col0 + (gi + 1) * POOL_GROUP] = (
                tot / cnt - tok).astype(BF16)


def _mix_pool_proj(src_ref, scol0, wpool_ref, pscale_ref, dst_ref, dcol0):
    for gi in range(len(POOL_WINDOWS)):
        lo, hi = gi * POOL_GROUP, (gi + 1) * POOL_GROUP
        ya = jnp.dot(src_ref[:, scol0 + lo:scol0 + hi], wpool_ref[gi].astype(BF16),
                     preferred_element_type=F32)
        dst_ref[:, dcol0 + lo:dcol0 + hi] = (ya * pscale_ref[:, lo:hi]).astype(BF16)


def _mix_sconv(z_ref, scext, wsc_ref, dst_ref, col0, tt, rc):
    C = D_BRANCH
    for r in range(0, tt, rc):
        rows = pl.ds(r, rc)
        for c0 in range(0, C, CCONV_LANES):
            lanes = slice(c0, c0 + CCONV_LANES)
            acc = None
            for k in range(SCONV_W):
                t = wsc_ref[k:k + 1, lanes] * scext[pl.ds(HALO + r - (SCONV_W - 1) + k, rc), lanes]
                acc = t if acc is None else acc + t
            bg = z_ref[rows, OFF_SC + C + c0:OFF_SC + C + c0 + CCONV_LANES].astype(F32)
            dst_ref[rows, col0 + c0:col0 + c0 + CCONV_LANES] = (bg * acc).astype(BF16)


def _mix_cconv(ccext, ccsh, wcc_ref, bcc_ref, gcln_ref, bcln_ref, dst_ref, col0, tt, rc):
    C = D_BRANCH
    sh_rows = ccsh.shape[1]
    for b in range(1, V7X_SUBLANES):
        for r in range(0, sh_rows, rc):
            n = min(rc, sh_rows - r)
            for c0 in range(0, C, CCONV_LANES):
                lanes = slice(c0, c0 + CCONV_LANES)
                ccsh[b - 1, pl.ds(r, n), lanes] = ccext[pl.ds(r + b, n), lanes]
    for r in range(0, tt, rc):
        parts = []
        for c0 in range(0, C, CCONV_LANES):
            lanes = slice(c0, c0 + CCONV_LANES)
            acc = None
            for k in range(CCONV_W):
                o = k + HALO - (CCONV_W - 1)
                o8, ob = (o // V7X_SUBLANES) * V7X_SUBLANES, o % V7X_SUBLANES
                rows = pl.ds(r + o8, rc)
                src = ccext[rows, lanes] if ob == 0 else ccsh[ob - 1, rows, lanes]
                t = wcc_ref[k:k + 1, lanes] * src
                acc = t if acc is None else acc + t
            parts.append(acc + bcc_ref[:, lanes])
        yl = _layernorm_rows(jnp.concatenate(parts, axis=1), gcln_ref[...], bcln_ref[...])
        dst_ref[pl.ds(r, rc), col0:col0 + C] = (yl * jax.nn.sigmoid(yl)).astype(BF16)


def _mix_v(z_ref, gvln_ref, bvln_ref, dst_ref, col0, v_ref, tt, rc):
    C = D_BRANCH
    for r in range(0, tt, rc):
        rows = pl.ds(r, rc)
        v = _layernorm_rows(z_ref[rows, OFF_GM + C:OFF_GM + 2 * C].astype(F32),
                            gvln_ref[...], bvln_ref[...])
        if v_ref is not None:
            v_ref[rows, :] = v
        dst_ref[rows, col0:col0 + C] = v.astype(BF16)


def _mix_gmlp(vb_ref, vcol0, u_ref, ucol0, ws_ref, bs_ref, dst_ref, dcol0, tt, chunk):
    ri = lax.broadcasted_iota(jnp.int32, (chunk, chunk), 0)
    ci = lax.broadcasted_iota(jnp.int32, (chunk, chunk), 1)
    for hh in range(N_GMLP_HEADS):
        lo, hi = hh * GMLP_HEAD, (hh + 1) * GMLP_HEAD
        wsm = jnp.where(ci <= ri, ws_ref[hh], 0.0).astype(BF16)
        for n in range(tt // chunk):
            rows = pl.ds(n * chunk, chunk)
            mixed = (jnp.dot(wsm, vb_ref[rows, vcol0 + lo:vcol0 + hi], preferred_element_type=F32)
                     + bs_ref[:, lo:hi])
            u = u_ref[rows, ucol0 + lo:ucol0 + hi].astype(F32)
            dst_ref[rows, dcol0 + lo:dcol0 + hi] = (u * mixed).astype(BF16)


def _mix_tail(exts, outs, tt):
    for ext, out in zip(exts, outs):
        tail = ext[tt:tt + HALO, :]
        ext[0:HALO, :] = tail
        out[0] = tail


def _mixer_kernel(*refs, tt, chunk, start_pos, has_state, emit_v):
    refs = list(refs)
    z_ref = refs.pop(0)
    state_refs = (refs.pop(0), refs.pop(0), refs.pop(0)) if has_state else None
    (wpool_ref, pscale_ref, wsc_ref, wcc_ref, bcc_ref, gcln_ref, bcln_ref, gvln_ref, bvln_ref,
     ws_ref, bs_ref) = refs[:11]
    refs = refs[11:]
    y_ref, npool_ref, nsc_ref, ncc_ref = refs[:4]
    refs = refs[4:]
    v_ref = refs.pop(0) if emit_v else None
    pext, scext, ccext, ccsh, pooled_s, vb_s = refs

    s = pl.program_id(1)
    C = D_BRANCH
    rc = min(MIX_RC, tt)
    exts = (pext, scext, ccext)

    _mix_reset(s == 0, exts, state_refs)
    _mix_stage(z_ref, exts, tt, rc)
    _mix_pool(pext, pooled_s, 0, start_pos + s * tt, tt, rc)
    _mix_pool_proj(pooled_s, 0, wpool_ref, pscale_ref, y_ref, 0)
    _mix_sconv(z_ref, scext, wsc_ref, y_ref, C, tt, rc)
    _mix_cconv(ccext, ccsh, wcc_ref, bcc_ref, gcln_ref, bcln_ref, y_ref, 2 * C, tt, rc)
    _mix_v(z_ref, gvln_ref, bvln_ref, vb_s, 0, v_ref, tt, rc)
    _mix_gmlp(vb_s, 0, z_ref, OFF_GM, ws_ref, bs_ref, y_ref, 3 * C, tt, chunk)
    _mix_tail(exts, (npool_ref, nsc_ref, ncc_ref), tt)


def _mixers(z, *, n_seq, t, tt, start_pos, states, lw, emit_v):
    m = z.shape[0]
    C = D_BRANCH
    assert m == n_seq * t and t % tt == 0 and tt >= HALO
    assert HALO >= CCONV_W - 1 and HALO % V7X_SUBLANES == 0 and tt % V7X_SUBLANES == 0
    chunk = min(t, GMLP_CHUNK)
    assert tt % chunk == 0
    nt = t // tt
    has_state = states is not None
    sh_rows = tt + HALO - V7X_SUBLANES

    def row_map(b, s):
        return (b * nt + s, 0)

    def const2(b, s):
        return (0, 0)

    def const3(b, s):
        return (0, 0, 0)

    def seq3(b, s):
        return (b, 0, 0)

    in_specs = [pl.BlockSpec((tt, MIX_COLS), row_map)]
    args = [z]
    if has_state:
        for st in states:
            in_specs.append(pl.BlockSpec((1, HALO, C), seq3))
            args.append(st)
    ws = lw["w_spatial"][:, :chunk, :chunk]
    bs = jnp.repeat(lw["b_spatial"][:, :chunk].T, GMLP_HEAD, axis=1)
    small = [
        (lw["w_pool"], const3), (lw["pool_scale"].reshape(1, C), const2),
        (lw["w_sconv"], const2), (lw["w_cconv"], const2), (lw["b_cconv"].reshape(1, C), const2),
        (lw["g_cln"].reshape(1, C), const2), (lw["b_cln"].reshape(1, C), const2),
        (lw["g_vln"].reshape(1, C), const2), (lw["b_vln"].reshape(1, C), const2),
        (ws, const3), (bs, const2),
    ]
    for arr, imap in small:
        in_specs.append(pl.BlockSpec(arr.shape, imap))
        args.append(arr)

    out_shape = [jax.ShapeDtypeStruct((m, D_MODEL), BF16)] + [
        jax.ShapeDtypeStruct((n_seq, HALO, C), F32)] * 3
    out_specs = [pl.BlockSpec((tt, D_MODEL), row_map)] + [pl.BlockSpec((1, HALO, C), seq3)] * 3
    if emit_v:
        out_shape.append(jax.ShapeDtypeStruct((m, C), F32))
        out_specs.append(pl.BlockSpec((tt, C), row_map))

    nbytes = (2 * (tt * MIX_COLS * 2 + tt * D_MODEL * 2 + tt * C * 4)
              + 3 * (tt + HALO) * C * 4 + (V7X_SUBLANES - 1) * sh_rows * C * 4 + 2 * tt * C * 2
              + 16 * HALO * C * 4
              + 16 * tt * C * 4)
    kern = functools.partial(_mixer_kernel, tt=tt, chunk=chunk, start_pos=start_pos,
                             has_state=has_state, emit_v=emit_v)
    return pl.pallas_call(
        kern,
        grid=(n_seq, nt),
        in_specs=in_specs,
        out_specs=out_specs,
        out_shape=out_shape,
        scratch_shapes=[pltpu.VMEM((HALO + tt, C), F32)] * 3
        + [pltpu.VMEM((V7X_SUBLANES - 1, sh_rows, C), F32)]
        + [pltpu.VMEM((tt, C), BF16)] * 2,
        compiler_params=pltpu.CompilerParams(
            dimension_semantics=("arbitrary", "arbitrary"), vmem_limit_bytes=_vmem_limit(nbytes)),
        name="mixers_state" if has_state else "mixers",
    )(*args)


def _pad_state(st):
    return jnp.pad(st, ((0, 0), (HALO - st.shape[1], 0), (0, 0)))


def _after_in_proj(x, z, wb, lw, g_next, *, n_seq, t, tt, start_pos, states, emit_v, ff1_side=()):
    tm = _row_tile(x.shape[0], TM_MAX)
    mix = _mixers(z, n_seq=n_seq, t=t, tt=tt, start_pos=start_pos, states=states, lw=lw,
                  emit_v=emit_v)
    y, new_states, v = mix[0], mix[1:4], (mix[4] if emit_v else None)
    merged = _branch_merge(y, z, wb["branch"], tm=tm, tn=512)
    x, xb, ss = _matmul_res_emit(merged, wb["out"], x, lw["g_ffn"], tm=tm, tn=1024, name="out_proj")
    out = _matmul_norm(xb, wb["ff1"], 0, ss, tm=tm, tn=1024, act="relu2", name="ff1", side=ff1_side)
    hid, casts = (out[0], list(out[1:])) if ff1_side else (out, [])
    main = _matmul_kacc(hid, wb["ff2"], x, g_next, tm=tm, tn=1024, tk=4096,
                        name="ff2" if g_next is not None else "ff2_last")
    x, xb, ss = (main[0], None, None) if g_next is None else main
    return x, xb, ss, new_states, v, casts


def kernel(x_prompt, x_sample, state_pool, state_sconv, state_cconv, g_mix, w_in, w_pool,
           pool_scale, w_sconv, w_cconv, b_cconv, g_cln, b_cln, g_vln, b_vln, w_spatial,
           b_spatial, w_branch, w_out, g_ffn, w_ff1, w_ff2, g_final):
    bp, tp, d = x_prompt.shape
    bs_, ts, _ = x_sample.shape
    assert d == D_MODEL and ts <= 64
    mp, ms = bp * tp, bs_ * ts
    tt_p = 256
    tt_s = ts

    xp, xs = x_prompt.reshape(mp, d), x_sample.reshape(ms, d)
    xbp, ssp = _prep(xp, g_mix[0], tr=_row_tile(mp, TR_MAX))
    xbs, sss = _prep(xs, g_mix[0], tr=_row_tile(ms, TR_MAX))

    tm_p, tm_s = _row_tile(mp, TM_MAX), _row_tile(ms, TM_MAX)
    w3 = {"in": w_in, "branch": w_branch.reshape(DEPTH, N_BRANCH * D_BRANCH, D_MODEL),
          "out": w_out, "ff1": w_ff1, "ff2": w_ff2}

    def store(wb_l, keys, casts):
        for key, c in zip(keys, casts):
            wb_l[key] = c.reshape(N_BRANCH, D_BRANCH, D_MODEL) if key == "branch" else c

    wb = [dict() for _ in range(DEPTH)]
    z_s, wb[0]["in"] = _matmul_norm(xbs, w_in, 0, sss, tm=tm_s, tn=768, name="in_proj")
    keys0 = ["branch", "out", "ff1", "ff2"]
    side0 = [(w3[k], 0) for k in keys0] + ([(w3["ff2"], 1)] if DEPTH > 1 else [])
    out = _matmul_norm(xbp, wb[0]["in"], 0, ssp, tm=tm_p, tn=1024, name="in_proj", side=side0)
    z_p = out[0]
    store(wb[0], keys0, out[1:1 + len(keys0)])
    if DEPTH > 1:
        wb[1]["ff2"] = out[1 + len(keys0)]

    outs_p, outs_s, vs = [], [], []
    for l in range(DEPTH):
        lw = dict(w_pool=w_pool[l], pool_scale=pool_scale[l], w_sconv=w_sconv[l],
                  w_cconv=w_cconv[l], b_cconv=b_cconv[l], g_cln=g_cln[l], b_cln=b_cln[l],
                  g_vln=g_vln[l], b_vln=b_vln[l], w_spatial=w_spatial[l], b_spatial=b_spatial[l],
                  g_ffn=g_ffn[l])
        last = l + 1 == DEPTH
        g_next = None if last else g_mix[l + 1]
        states = (_pad_state(state_pool[l]), _pad_state(state_sconv[l]), _pad_state(state_cconv[l]))
        keys_next = [] if last else ["in", "branch", "out", "ff1"] + (["ff2"] if l + 1 > 1 else [])
        xs, xbs, sss, st_s, v, _ = _after_in_proj(
            xs, z_s, wb[l], lw, g_next, n_seq=bs_, t=ts, tt=tt_s, start_pos=PAST_LEN, states=states,
            emit_v=True)
        xp, xbp, ssp, st_p, _, casts = _after_in_proj(
            xp, z_p, wb[l], lw, g_next, n_seq=bp, t=tp, tt=tt_p, start_pos=0, states=None,
            emit_v=False, ff1_side=[(w3[k], l + 1) for k in keys_next])
        outs_p.append(st_p)
        outs_s.append(st_s)
        vs.append(v.reshape(bs_, ts, D_BRANCH))
        if not last:
            store(wb[l + 1], keys_next, casts)
            z_s = _matmul_norm(xbs, wb[l + 1]["in"], 0, sss, tm=tm_s, tn=1024, name="in_proj")
            z_p = _matmul_norm(xbp, wb[l + 1]["in"], 0, ssp, tm=tm_p, tn=1024, name="in_proj")

    y_prompt = _rmsnorm(xp, g_final, tr=_row_tile(mp, TR_MAX))
    y_sample = _rmsnorm(xs, g_final, tr=_row_tile(ms, TR_MAX))

    def stack(outs, idx, keep):
        return jnp.stack([o[idx][:, HALO - keep:] for o in outs])

    return (y_prompt.reshape(bp, tp, d), y_sample.reshape(bs_, ts, d),
            stack(outs_p, 0, POOL_STATE), stack(outs_s, 0, POOL_STATE),
            stack(outs_p, 1, SCONV_W - 1), stack(outs_s, 1, SCONV_W - 1),
            stack(outs_p, 2, CCONV_W - 1), stack(outs_s, 2, CCONV_W - 1),
            jnp.stack(vs))
```

```python
import functools

import jax
import jax.numpy as jnp
from jax import lax
from jax.experimental import pallas as pl
from jax.experimental.pallas import tpu as pltpu

F32 = jnp.float32
BF16 = jnp.bfloat16

D_MODEL = 4096
DEPTH = 2
PAST_LEN = 2048
N_BRANCH = 4
D_BRANCH = D_MODEL // N_BRANCH
POOL_WINDOWS = (2, 4, 8, 16)
POOL_GROUP = D_BRANCH // len(POOL_WINDOWS)
POOL_STATE = max(POOL_WINDOWS) - 1
SCONV_W = 3
CCONV_W = 31
GMLP_CHUNK = 128
N_GMLP_HEADS = 8
GMLP_HEAD = D_BRANCH // N_GMLP_HEADS
D_FF = 4 * D_MODEL
EPS = 1e-6
OFF_POOL = 0
OFF_SC = OFF_POOL + D_BRANCH
OFF_CC = OFF_SC + 3 * D_BRANCH
OFF_GM = OFF_CC + 2 * D_BRANCH
OFF_GATE = OFF_GM + 2 * D_BRANCH
IN_COLS = OFF_GATE + N_BRANCH * D_MODEL
MIX_COLS = OFF_GATE

V7X_VMEM_BYTES = 64 * 1024 * 1024
V7X_LANES = 128
V7X_SUBLANES = 8
V7X_SUBLANES_BF16 = 16
MiB = 1024 * 1024

HALO = 32
MIX_RC = 32
CCONV_LANES = 128
TM_MAX = 1024
TR_MAX = 256
SIDE_COLS = 16


def _vmem_limit(nbytes):
    return int(min(V7X_VMEM_BYTES - 2 * MiB, nbytes + 12 * MiB))


def _row_tile(m, target):
    best = None
    for t in range(V7X_SUBLANES_BF16, target + 1, V7X_SUBLANES_BF16):
        if m % t == 0:
            best = t
    assert best is not None, (m, target)
    return best


def _fold_lanes(v):
    acc = v[:, 0:V7X_LANES]
    for c in range(1, v.shape[1] // V7X_LANES):
        acc = acc + v[:, c * V7X_LANES:(c + 1) * V7X_LANES]
    return acc


def _row_scale(ss_ref):
    return lax.rsqrt(jnp.sum(ss_ref[...], axis=-1, keepdims=True) * (1.0 / D_MODEL) + EPS)


def _emit_stream(x, g_ref, xb_ref, ss_ref, first):
    xb_ref[...] = (x * g_ref[...]).astype(BF16)
    part = _fold_lanes(x * x)

    @pl.when(first)
    def _():
        ss_ref[...] = part

    @pl.when(jnp.logical_not(first))
    def _():
        ss_ref[...] += part


def _weight_spec(w, layer, blk, idx):
    if w.dtype == BF16:
        return pl.BlockSpec(blk, idx)
    return pl.BlockSpec((None,) + blk, lambda *g: (layer,) + idx(*g))


def _load_weight(w_ref, wb_ref):
    w = w_ref[...]
    if wb_ref is not None:
        w = w.astype(BF16)
        wb_ref[...] = w
    return w


def _prep_kernel(x_ref, g_ref, xb_ref, ss_ref):
    x = x_ref[...]
    xb_ref[...] = (x * g_ref[...]).astype(BF16)
    ss_ref[...] = _fold_lanes(x * x)


def _prep(x, g, *, tr):
    m, d = x.shape
    assert m % tr == 0
    nbytes = 2 * tr * d * (4 + 2) + 2 * tr * d * 4
    return pl.pallas_call(
        _prep_kernel,
        grid=(m // tr,),
        in_specs=[pl.BlockSpec((tr, d), lambda i: (i, 0)),
                  pl.BlockSpec((1, d), lambda i: (0, 0))],
        out_specs=[pl.BlockSpec((tr, d), lambda i: (i, 0)),
                   pl.BlockSpec((tr, V7X_LANES), lambda i: (i, 0))],
        out_shape=[jax.ShapeDtypeStruct((m, d), BF16), jax.ShapeDtypeStruct((m, V7X_LANES), F32)],
        compiler_params=pltpu.CompilerParams(
            dimension_semantics=("arbitrary",), vmem_limit_bytes=_vmem_limit(nbytes)),
        name="prep",
    )(x, g.reshape(1, d))


def _rmsnorm_kernel(x_ref, g_ref, o_ref):
    x = x_ref[...]
    ms = jnp.mean(x * x, axis=-1, keepdims=True)
    o_ref[...] = (x * lax.rsqrt(ms + EPS) * g_ref[...]).astype(o_ref.dtype)


def _rmsnorm(x, g, *, tr):
    m, d = x.shape
    assert m % tr == 0
    nbytes = 2 * tr * d * (4 + 4) + 2 * tr * d * 4
    return pl.pallas_call(
        _rmsnorm_kernel,
        grid=(m // tr,),
        in_specs=[pl.BlockSpec((tr, d), lambda i: (i, 0)),
                  pl.BlockSpec((1, d), lambda i: (0, 0))],
        out_specs=pl.BlockSpec((tr, d), lambda i: (i, 0)),
        out_shape=jax.ShapeDtypeStruct((m, d), F32),
        compiler_params=pltpu.CompilerParams(
            dimension_semantics=("arbitrary",), vmem_limit_bytes=_vmem_limit(nbytes)),
        name="rmsnorm",
    )(x, g.reshape(1, d))


def _mm_norm_kernel(*refs, act, cast, n_side):
    a_ref, w_ref, ss_ref = refs[:3]
    side_in = refs[3:3 + n_side]
    o_ref = refs[3 + n_side]
    wb_ref = refs[4 + n_side] if cast else None
    side_out = refs[len(refs) - n_side:] if n_side else ()
    for s_in, s_out in zip(side_in, side_out):
        s_out[...] = s_in[...].astype(BF16)
    acc = jnp.dot(a_ref[...], _load_weight(w_ref, wb_ref), preferred_element_type=F32)
    acc = acc * _row_scale(ss_ref)
    if act == "relu2":
        acc = jnp.square(jnp.maximum(acc, 0.0))
    o_ref[...] = acc.astype(o_ref.dtype)


def _matmul_norm(a, w, layer, ss, *, tm, tn, act=None, name, side=()):
    m, k = a.shape
    k2, n = w.shape[-2:]
    cast = w.dtype != BF16
    assert k == k2 and m % tm == 0 and n % tn == 0 and (not cast or m == tm)
    ni, nj = m // tm, n // tn
    wsz = 4 if cast else 2
    nbytes = (2 * (tm * k * 2 + k * tn * wsz + tm * tn * 2 + tm * V7X_LANES * 4) + tm * tn * 4
              + (3 * k * tn * 2 if cast else 0))
    in_specs = [pl.BlockSpec((tm, k), lambda i, j: (i, 0)),
                _weight_spec(w, layer, (k, tn), lambda i, j: (0, j)),
                pl.BlockSpec((tm, V7X_LANES), lambda i, j: (i, 0))]
    args = [a, w, ss]
    out_specs = [pl.BlockSpec((tm, tn), lambda i, j: (i, j))]
    out_shape = [jax.ShapeDtypeStruct((m, n), BF16)]
    if cast:
        out_specs.append(pl.BlockSpec((k, tn), lambda i, j: (0, j)))
        out_shape.append(jax.ShapeDtypeStruct((k, n), BF16))
    side_cols = min(nj, SIDE_COLS)
    for sw, sl in side:
        _, r, c = sw.shape
        assert r % (ni * side_cols) == 0
        rb = r // (ni * side_cols)
        assert rb % V7X_SUBLANES_BF16 == 0
        in_specs.append(pl.BlockSpec(
            (None, rb, c), lambda i, j, sl=sl: (sl, i * side_cols + jnp.minimum(j, side_cols - 1), 0)))
        out_specs.append(pl.BlockSpec(
            (rb, c), lambda i, j: (i * side_cols + jnp.minimum(j, side_cols - 1), 0)))
        out_shape.append(jax.ShapeDtypeStruct((r, c), BF16))
        args.append(sw)
        nbytes += 2 * rb * c * (4 + 2)
    out = pl.pallas_call(
        functools.partial(_mm_norm_kernel, act=act, cast=cast, n_side=len(side)),
        grid=(ni, nj),
        in_specs=in_specs,
        out_specs=out_specs,
        out_shape=out_shape,
        compiler_params=pltpu.CompilerParams(
            dimension_semantics=("arbitrary", "arbitrary"), vmem_limit_bytes=_vmem_limit(nbytes)),
        name=name,
    )(*args)
    return out if (cast or side) else out[0]


def _mm_res_emit_kernel(a_ref, w_ref, r_ref, g_ref, o_ref, xb_ref, ss_ref):
    x = r_ref[...] + jnp.dot(a_ref[...], w_ref[...], preferred_element_type=F32)
    o_ref[...] = x
    _emit_stream(x, g_ref, xb_ref, ss_ref, pl.program_id(1) == 0)


def _matmul_res_emit(a, w, res, g, *, tm, tn, name):
    m, k = a.shape
    k2, n = w.shape
    assert k == k2 and m % tm == 0 and n % tn == 0
    nbytes = (2 * (tm * k * 2 + k * tn * 2 + tm * tn * (4 + 4 + 2) + tm * V7X_LANES * 4)
              + 2 * tm * tn * 4)
    return pl.pallas_call(
        _mm_res_emit_kernel,
        grid=(m // tm, n // tn),
        in_specs=[pl.BlockSpec((tm, k), lambda i, j: (i, 0)),
                  pl.BlockSpec((k, tn), lambda i, j: (0, j)),
                  pl.BlockSpec((tm, tn), lambda i, j: (i, j)),
                  pl.BlockSpec((1, tn), lambda i, j: (0, j))],
        out_specs=[pl.BlockSpec((tm, tn), lambda i, j: (i, j)),
                   pl.BlockSpec((tm, tn), lambda i, j: (i, j)),
                   pl.BlockSpec((tm, V7X_LANES), lambda i, j: (i, 0))],
        out_shape=[jax.ShapeDtypeStruct((m, n), F32), jax.ShapeDtypeStruct((m, n), BF16),
                   jax.ShapeDtypeStruct((m, V7X_LANES), F32)],
        compiler_params=pltpu.CompilerParams(
            dimension_semantics=("arbitrary", "arbitrary"), vmem_limit_bytes=_vmem_limit(nbytes)),
        name=name,
    )(a, w, res, g.reshape(1, n))


def _mm_kacc_kernel(a_ref, w_ref, r_ref, *rest, emit):
    rest = list(rest)
    g_ref = rest.pop(0) if emit else None
    o_ref = rest.pop(0)
    xb_ref, ss_ref = (rest.pop(0), rest.pop(0)) if emit else (None, None)
    kk = pl.program_id(2)

    @pl.when(kk == 0)
    def _():
        o_ref[...] = r_ref[...]

    o_ref[...] += jnp.dot(a_ref[...], w_ref[...], preferred_element_type=F32)

    if emit:
        @pl.when(kk == pl.num_programs(2) - 1)
        def _():
            _emit_stream(o_ref[...], g_ref, xb_ref, ss_ref, pl.program_id(1) == 0)


def _matmul_kacc(a, w, res, g, *, tm, tn, tk, name):
    m, k = a.shape
    k2, n = w.shape
    emit = g is not None
    assert k == k2 and m % tm == 0 and n % tn == 0 and k % tk == 0
    nbytes = 2 * (tm * tk * 2 + tk * tn * 2 + 2 * tm * tn * 4) + 2 * tm * tn * 4
    in_specs = [pl.BlockSpec((tm, tk), lambda i, j, kk: (i, kk)),
                pl.BlockSpec((tk, tn), lambda i, j, kk: (kk, j)),
                pl.BlockSpec((tm, tn), lambda i, j, kk: (i, j))]
    args = [a, w, res]
    out_specs = [pl.BlockSpec((tm, tn), lambda i, j, kk: (i, j))]
    out_shape = [jax.ShapeDtypeStruct((m, n), F32)]
    if emit:
        in_specs.append(pl.BlockSpec((1, tn), lambda i, j, kk: (0, j)))
        args.append(g.reshape(1, n))
        out_specs += [pl.BlockSpec((tm, tn), lambda i, j, kk: (i, j)),
                      pl.BlockSpec((tm, V7X_LANES), lambda i, j, kk: (i, 0))]
        out_shape += [jax.ShapeDtypeStruct((m, n), BF16), jax.ShapeDtypeStruct((m, V7X_LANES), F32)]
        nbytes += 2 * (tm * tn * 2 + tm * V7X_LANES * 4)
    return pl.pallas_call(
        functools.partial(_mm_kacc_kernel, emit=emit),
        grid=(m // tm, n // tn, k // tk),
        in_specs=in_specs,
        out_specs=out_specs,
        out_shape=out_shape,
        compiler_params=pltpu.CompilerParams(
            dimension_semantics=("arbitrary", "arbitrary", "arbitrary"),
            vmem_limit_bytes=_vmem_limit(nbytes)),
        name=name,
    )(*args)


def _branch_kernel(y_ref, g0_ref, g1_ref, g2_ref, g3_ref, wb_ref, o_ref):
    acc = None
    for i, g_ref in enumerate((g0_ref, g1_ref, g2_ref, g3_ref)):
        d = jnp.dot(y_ref[:, i * D_BRANCH:(i + 1) * D_BRANCH], wb_ref[i],
                    preferred_element_type=F32)
        t = jax.nn.sigmoid(g_ref[...].astype(F32)) * d
        acc = t if acc is None else acc + t
    o_ref[...] = acc.astype(o_ref.dtype)


def _branch_merge(y, z, wb, *, tm, tn):
    m = y.shape[0]
    assert m % tm == 0 and D_MODEL % tn == 0 and OFF_GATE % tn == 0
    nbytes = (2 * (tm * D_MODEL * 2 + N_BRANCH * tm * tn * 2 + N_BRANCH * D_BRANCH * tn * 2
                   + tm * tn * 2) + 3 * tm * tn * 4)

    def gate_spec(i):
        base = (OFF_GATE + i * D_MODEL) // tn
        return pl.BlockSpec((tm, tn), lambda r, j: (r, base + j))

    return pl.pallas_call(
        _branch_kernel,
        grid=(m // tm, D_MODEL // tn),
        in_specs=[pl.BlockSpec((tm, D_MODEL), lambda r, j: (r, 0), pipeline_mode=pl.Buffered(1))]
        + [gate_spec(i) for i in range(N_BRANCH)]
        + [pl.BlockSpec((N_BRANCH, D_BRANCH, tn), lambda r, j: (0, 0, j))],
        out_specs=pl.BlockSpec((tm, tn), lambda r, j: (r, j)),
        out_shape=jax.ShapeDtypeStruct((m, D_MODEL), BF16),
        compiler_params=pltpu.CompilerParams(
            dimension_semantics=("arbitrary", "arbitrary"), vmem_limit_bytes=_vmem_limit(nbytes)),
        name="branch_merge",
    )(y, z, z, z, z, wb)


def _layernorm_rows(x, g, b):
    mu = jnp.mean(x, axis=-1, keepdims=True)
    xc = x - mu
    var = jnp.mean(xc * xc, axis=-1, keepdims=True)
    return xc * lax.rsqrt(var + EPS) * g + b


def _mix_reset(first, exts, state_refs):
    @pl.when(first)
    def _():
        for idx, ext in enumerate(exts):
            if state_refs is None:
                ext[0:HALO, :] = jnp.zeros((HALO, D_BRANCH), F32)
            else:
                ext[0:HALO, :] = state_refs[idx][0]


def _mix_stage(z_ref, exts, tt, rc):
    pext, scext, ccext = exts
    C = D_BRANCH
    for r in range(0, tt, rc):
        rows = pl.ds(r, rc)
        dst = pl.ds(HALO + r, rc)
        for c0 in range(0, C, CCONV_LANES):
            c1 = c0 + CCONV_LANES
            pext[dst, c0:c1] = z_ref[rows, OFF_POOL + c0:OFF_POOL + c1].astype(F32)
            h = z_ref[rows, OFF_SC + c0:OFF_SC + c1].astype(F32)
            cg = z_ref[rows, OFF_SC + 2 * C + c0:OFF_SC + 2 * C + c1].astype(F32)
            scext[dst, c0:c1] = cg * h
            a = z_ref[rows, OFF_CC + c0:OFF_CC + c1].astype(F32)
            g = z_ref[rows, OFF_CC + C + c0:OFF_CC + C + c1].astype(F32)
            ccext[dst, c0:c1] = a * jax.nn.sigmoid(g)


def _mix_pool(pext, dst_ref, col0, pos0, tt, rc):
    for r in range(0, tt, rc):
        pos = pos0 + r + lax.broadcasted_iota(jnp.int32, (rc, POOL_GROUP), 0)
        for gi, w in enumerate(POOL_WINDOWS):
            lanes = slice(gi * POOL_GROUP, (gi + 1) * POOL_GROUP)
            tok = pext[pl.ds(HALO + r, rc), lanes]
            tot = tok
            for k in range(1, w):
                tot = tot + pext[pl.ds(HALO + r - k, rc), lanes]
            cnt = jnp.minimum(pos + 1, w).astype(F32)
            dst_ref[pl.ds(r, rc), col0 + gi * POOL_GROUP:col0 + (gi + 1) * POOL_GROUP] = (
                tot / cnt - tok).astype(BF16)


def _mix_pool_proj(src_ref, scol0, wpool_ref, pscale_ref, dst_ref, dcol0):
    for gi in range(len(POOL_WINDOWS)):
        lo, hi = gi * POOL_GROUP, (gi + 1) * POOL_GROUP
        ya = jnp.dot(src_ref[:, scol0 + lo:scol0 + hi], wpool_ref[gi].astype(BF16),
                     preferred_element_type=F32)
        dst_ref[:, dcol0 + lo:dcol0 + hi] = (ya * pscale_ref[:, lo:hi]).astype(BF16)


def _mix_sconv(z_ref, scext, wsc_ref, dst_ref, col0, tt, rc):
    C = D_BRANCH
    for r in range(0, tt, rc):
        rows = pl.ds(r, rc)
        for c0 in range(0, C, CCONV_LANES):
            lanes = slice(c0, c0 + CCONV_LANES)
            acc = None
            for k in range(SCONV_W):
                t = wsc_ref[k:k + 1, lanes] * scext[pl.ds(HALO + r - (SCONV_W - 1) + k, rc), lanes]
                acc = t if acc is None else acc + t
            bg = z_ref[rows, OFF_SC + C + c0:OFF_SC + C + c0 + CCONV_LANES].astype(F32)
            dst_ref[rows, col0 + c0:col0 + c0 + CCONV_LANES] = (bg * acc).astype(BF16)


def _mix_cconv(ccext, ccsh, wcc_ref, bcc_ref, gcln_ref, bcln_ref, dst_ref, col0, tt, rc):
    C = D_BRANCH
    sh_rows = ccsh.shape[1]
    for b in range(1, V7X_SUBLANES):
        for r in range(0, sh_rows, rc):
            n = min(rc, sh_rows - r)
            for c0 in range(0, C, CCONV_LANES):
                lanes = slice(c0, c0 + CCONV_LANES)
                ccsh[b - 1, pl.ds(r, n), lanes] = ccext[pl.ds(r + b, n), lanes]
    for r in range(0, tt, rc):
        parts = []
        for c0 in range(0, C, CCONV_LANES):
            lanes = slice(c0, c0 + CCONV_LANES)
            acc = None
            for k in range(CCONV_W):
                o = k + HALO - (CCONV_W - 1)
                o8, ob = (o // V7X_SUBLANES) * V7X_SUBLANES, o % V7X_SUBLANES
                rows = pl.ds(r + o8, rc)
                src = ccext[rows, lanes] if ob == 0 else ccsh[ob - 1, rows, lanes]
                t = wcc_ref[k:k + 1, lanes] * src
                acc = t if acc is None else acc + t
            parts.append(acc + bcc_ref[:, lanes])
        yl = _layernorm_rows(jnp.concatenate(parts, axis=1), gcln_ref[...], bcln_ref[...])
        dst_ref[pl.ds(r, rc), col0:col0 + C] = (yl * jax.nn.sigmoid(yl)).astype(BF16)


def _mix_v(z_ref, gvln_ref, bvln_ref, dst_ref, col0, v_ref, tt, rc):
    C = D_BRANCH
    for r in range(0, tt, rc):
        rows = pl.ds(r, rc)
        v = _layernorm_rows(z_ref[rows, OFF_GM + C:OFF_GM + 2 * C].astype(F32),
                            gvln_ref[...], bvln_ref[...])
        if v_ref is not None:
            v_ref[rows, :] = v
        dst_ref[rows, col0:col0 + C] = v.astype(BF16)


def _mix_gmlp(vb_ref, vcol0, u_ref, ucol0, ws_ref, bs_ref, dst_ref, dcol0, tt, chunk):
    ri = lax.broadcasted_iota(jnp.int32, (chunk, chunk), 0)
    ci = lax.broadcasted_iota(jnp.int32, (chunk, chunk), 1)
    for hh in range(N_GMLP_HEADS):
        lo, hi = hh * GMLP_HEAD, (hh + 1) * GMLP_HEAD
        wsm = jnp.where(ci <= ri, ws_ref[hh], 0.0).astype(BF16)
        for n in range(tt // chunk):
            rows = pl.ds(n * chunk, chunk)
            mixed = (jnp.dot(wsm, vb_ref[rows, vcol0 + lo:vcol0 + hi], preferred_element_type=F32)
                     + bs_ref[:, lo:hi])
            u = u_ref[rows, ucol0 + lo:ucol0 + hi].astype(F32)
            dst_ref[rows, dcol0 + lo:dcol0 + hi] = (u * mixed).astype(BF16)


def _mix_tail(exts, outs, tt):
    for ext, out in zip(exts, outs):
        tail = ext[tt:tt + HALO, :]
        ext[0:HALO, :] = tail
        out[0] = tail


def _mixer_kernel(*refs, tt, chunk, start_pos, has_state, emit_v):
    refs = list(refs)
    z_ref = refs.pop(0)
    state_refs = (refs.pop(0), refs.pop(0), refs.pop(0)) if has_state else None
    (wpool_ref, pscale_ref, wsc_ref, wcc_ref, bcc_ref, gcln_ref, bcln_ref, gvln_ref, bvln_ref,
     ws_ref, bs_ref) = refs[:11]
    refs = refs[11:]
    y_ref, npool_ref, nsc_ref, ncc_ref = refs[:4]
    refs = refs[4:]
    v_ref = refs.pop(0) if emit_v else None
    pext, scext, ccext, ccsh, pooled_s, vb_s = refs

    s = pl.program_id(1)
    C = D_BRANCH
    rc = min(MIX_RC, tt)
    exts = (pext, scext, ccext)

    _mix_reset(s == 0, exts, state_refs)
    _mix_stage(z_ref, exts, tt, rc)
    _mix_pool(pext, pooled_s, 0, start_pos + s * tt, tt, rc)
    _mix_pool_proj(pooled_s, 0, wpool_ref, pscale_ref, y_ref, 0)
    _mix_sconv(z_ref, scext, wsc_ref, y_ref, C, tt, rc)
    _mix_cconv(ccext, ccsh, wcc_ref, bcc_ref, gcln_ref, bcln_ref, y_ref, 2 * C, tt, rc)
    _mix_v(z_ref, gvln_ref, bvln_ref, vb_s, 0, v_ref, tt, rc)
    _mix_gmlp(vb_s, 0, z_ref, OFF_GM, ws_ref, bs_ref, y_ref, 3 * C, tt, chunk)
    _mix_tail(exts, (npool_ref, nsc_ref, ncc_ref), tt)


def _mixers(z, *, n_seq, t, tt, start_pos, states, lw, emit_v):
    m = z.shape[0]
    C = D_BRANCH
    assert m == n_seq * t and t % tt == 0 and tt >= HALO
    assert HALO >= CCONV_W - 1 and HALO % V7X_SUBLANES == 0 and tt % V7X_SUBLANES == 0
    chunk = min(t, GMLP_CHUNK)
    assert tt % chunk == 0
    nt = t // tt
    has_state = states is not None
    sh_rows = tt + HALO - V7X_SUBLANES

    def row_map(b, s):
        return (b * nt + s, 0)

    def const2(b, s):
        return (0, 0)

    def const3(b, s):
        return (0, 0, 0)

    def seq3(b, s):
        return (b, 0, 0)

    in_specs = [pl.BlockSpec((tt, MIX_COLS), row_map)]
    args = [z]
    if has_state:
        for st in states:
            in_specs.append(pl.BlockSpec((1, HALO, C), seq3))
            args.append(st)
    ws = lw["w_spatial"][:, :chunk, :chunk]
    bs = jnp.repeat(lw["b_spatial"][:, :chunk].T, GMLP_HEAD, axis=1)
    small = [
        (lw["w_pool"], const3), (lw["pool_scale"].reshape(1, C), const2),
        (lw["w_sconv"], const2), (lw["w_cconv"], const2), (lw["b_cconv"].reshape(1, C), const2),
        (lw["g_cln"].reshape(1, C), const2), (lw["b_cln"].reshape(1, C), const2),
        (lw["g_vln"].reshape(1, C), const2), (lw["b_vln"].reshape(1, C), const2),
        (ws, const3), (bs, const2),
    ]
    for arr, imap in small:
        in_specs.append(pl.BlockSpec(arr.shape, imap))
        args.append(arr)

    out_shape = [jax.ShapeDtypeStruct((m, D_MODEL), BF16)] + [
        jax.ShapeDtypeStruct((n_seq, HALO, C), F32)] * 3
    out_specs = [pl.BlockSpec((tt, D_MODEL), row_map)] + [pl.BlockSpec((1, HALO, C), seq3)] * 3
    if emit_v:
        out_shape.append(jax.ShapeDtypeStruct((m, C), F32))
        out_specs.append(pl.BlockSpec((tt, C), row_map))

    nbytes = (2 * (tt * MIX_COLS * 2 + tt * D_MODEL * 2 + tt * C * 4)
              + 3 * (tt + HALO) * C * 4 + (V7X_SUBLANES - 1) * sh_rows * C * 4 + 2 * tt * C * 2
              + 16 * HALO * C * 4
              + 16 * tt * C * 4)
    kern = functools.partial(_mixer_kernel, tt=tt, chunk=chunk, start_pos=start_pos,
                             has_state=has_state, emit_v=emit_v)
    return pl.pallas_call(
        kern,
        grid=(n_seq, nt),
        in_specs=in_specs,
        out_specs=out_specs,
        out_shape=out_shape,
        scratch_shapes=[pltpu.VMEM((HALO + tt, C), F32)] * 3
        + [pltpu.VMEM((V7X_SUBLANES - 1, sh_rows, C), F32)]
        + [pltpu.VMEM((tt, C), BF16)] * 2,
        compiler_params=pltpu.CompilerParams(
            dimension_semantics=("arbitrary", "arbitrary"), vmem_limit_bytes=_vmem_limit(nbytes)),
        name="mixers_state" if has_state else "mixers",
    )(*args)


def _pad_state(st):
    return jnp.pad(st, ((0, 0), (HALO - st.shape[1], 0), (0, 0)))


def _after_in_proj(x, z, wb, lw, g_next, *, n_seq, t, tt, start_pos, states, emit_v, ff1_side=()):
    tm = _row_tile(x.shape[0], TM_MAX)
    mix = _mixers(z, n_seq=n_seq, t=t, tt=tt, start_pos=start_pos, states=states, lw=lw,
                  emit_v=emit_v)
    y, new_states, v = mix[0], mix[1:4], (mix[4] if emit_v else None)
    merged = _branch_merge(y, z, wb["branch"], tm=tm, tn=1024)
    x, xb, ss = _matmul_res_emit(merged, wb["out"], x, lw["g_ffn"], tm=tm, tn=1024, name="out_proj")
    out = _matmul_norm(xb, wb["ff1"], 0, ss, tm=tm, tn=1024, act="relu2", name="ff1", side=ff1_side)
    hid, casts = (out[0], list(out[1:])) if ff1_side else (out, [])
    main = _matmul_kacc(hid, wb["ff2"], x, g_next, tm=tm, tn=1024, tk=4096,
                        name="ff2" if g_next is not None else "ff2_last")
    x, xb, ss = (main[0], None, None) if g_next is None else main
    return x, xb, ss, new_states, v, casts


def kernel(x_prompt, x_sample, state_pool, state_sconv, state_cconv, g_mix, w_in, w_pool,
           pool_scale, w_sconv, w_cconv, b_cconv, g_cln, b_cln, g_vln, b_vln, w_spatial,
           b_spatial, w_branch, w_out, g_ffn, w_ff1, w_ff2, g_final):
    bp, tp, d = x_prompt.shape
    bs_, ts, _ = x_sample.shape
    assert d == D_MODEL and ts <= 64
    mp, ms = bp * tp, bs_ * ts
    tt_p = 256
    tt_s = ts

    xp, xs = x_prompt.reshape(mp, d), x_sample.reshape(ms, d)
    xbp, ssp = _prep(xp, g_mix[0], tr=_row_tile(mp, TR_MAX))
    xbs, sss = _prep(xs, g_mix[0], tr=_row_tile(ms, TR_MAX))

    tm_p, tm_s = _row_tile(mp, TM_MAX), _row_tile(ms, TM_MAX)
    w3 = {"in": w_in, "branch": w_branch.reshape(DEPTH, N_BRANCH * D_BRANCH, D_MODEL),
          "out": w_out, "ff1": w_ff1, "ff2": w_ff2}

    def store(wb_l, keys, casts):
        for key, c in zip(keys, casts):
            wb_l[key] = c.reshape(N_BRANCH, D_BRANCH, D_MODEL) if key == "branch" else c

    wb = [dict() for _ in range(DEPTH)]
    z_s, wb[0]["in"] = _matmul_norm(xbs, w_in, 0, sss, tm=tm_s, tn=768, name="in_proj")
    keys0 = ["branch", "out", "ff1", "ff2"]
    side0 = [(w3[k], 0) for k in keys0] + ([(w3["ff2"], 1)] if DEPTH > 1 else [])
    out = _matmul_norm(xbp, wb[0]["in"], 0, ssp, tm=tm_p, tn=1024, name="in_proj", side=side0)
    z_p = out[0]
    store(wb[0], keys0, out[1:1 + len(keys0)])
    if DEPTH > 1:
        wb[1]["ff2"] = out[1 + len(keys0)]

    outs_p, outs_s, vs = [], [], []
    for l in range(DEPTH):
        lw = dict(w_pool=w_pool[l], pool_scale=pool_scale[l], w_sconv=w_sconv[l],
                  w_cconv=w_cconv[l], b_cconv=b_cconv[l], g_cln=g_cln[l], b_cln=b_cln[l],
                  g_vln=g_vln[l], b_vln=b_vln[l], w_spatial=w_spatial[l], b_spatial=b_spatial[l],
                  g_ffn=g_ffn[l])
        last = l + 1 == DEPTH
        g_next = None if last else g_mix[l + 1]
        states = (_pad_state(state_pool[l]), _pad_state(state_sconv[l]), _pad_state(state_cconv[l]))
        keys_next = [] if last else ["in", "branch", "out", "ff1"] + (["ff2"] if l + 1 > 1 else [])
        xs, xbs, sss, st_s, v, _ = _after_in_proj(
            xs, z_s, wb[l], lw, g_next, n_seq=bs_, t=ts, tt=tt_s, start_pos=PAST_LEN, states=states,
            emit_v=True)
        xp, xbp, ssp, st_p, _, casts = _after_in_proj(
            xp, z_p, wb[l], lw, g_next, n_seq=bp, t=tp, tt=tt_p, start_pos=0, states=None,
            emit_v=False, ff1_side=[(w3[k], l + 1) for k in keys_next])
        outs_p.append(st_p)
        outs_s.append(st_s)
        vs.append(v.reshape(bs_, ts, D_BRANCH))
        if not last:
            store(wb[l + 1], keys_next, casts)
            z_s = _matmul_norm(xbs, wb[l + 1]["in"], 0, sss, tm=tm_s, tn=1024, name="in_proj")
            z_p = _matmul_norm(xbp, wb[l + 1]["in"], 0, ssp, tm=tm_p, tn=1024, name="in_proj")

    y_prompt = _rmsnorm(xp, g_final, tr=_row_tile(mp, TR_MAX))
    y_sample = _rmsnorm(xs, g_final, tr=_row_tile(ms, TR_MAX))

    def stack(outs, idx, keep):
        return jnp.stack([o[idx][:, HALO - keep:] for o in outs])

    return (y_prompt.reshape(bp, tp, d), y_sample.reshape(bs_, ts, d),
            stack(outs_p, 0, POOL_STATE), stack(outs_s, 0, POOL_STATE),
            stack(outs_p, 1, SCONV_W - 1), stack(outs_s, 1, SCONV_W - 1),
            stack(outs_p, 2, CCONV_W - 1), stack(outs_s, 2, CCONV_W - 1),
            jnp.stack(vs))
```
